```python
import jax, jax.numpy as jnp
from jax import lax
import numpy as np

D_MODEL = 4096
BATCH = 16
SEQ = 256
DEPTH = 2
DEC_BATCH = 2
DEC_SEQ = 1024
PAST_LEN = 256

GRID_W = 64
NA_HEADS = 16
HEAD_DIM = 128
NA_DIM = NA_HEADS * HEAD_DIM
WIN_R = 8
WIN_C = 16
SGU_DIM = D_MODEL - NA_DIM
SGU_GROUPS = 16
SGU_GC = SGU_DIM // SGU_GROUPS
SGU_CHUNK = 128
MIX_IN = 3 * NA_DIM + 2 * SGU_DIM
CONV_DIM = D_MODEL
CONV_K = 31
D_FF = 11008
N_MOD = 9
QUERY_BLOCK = 128
LN_EPS = 1e-5
ALPHA = (2 * DEPTH) ** 0.25
BETA = (8 * DEPTH) ** -0.25
NEG_INF = -1e30

kernel_name = "hybrid_natten_sgu_conformer_diffusion_step"


def layer_norm(x, g, b):
    xf = x.astype(jnp.float32)
    mu = jnp.mean(xf, axis=-1, keepdims=True)
    var = jnp.mean(jnp.square(xf - mu), axis=-1, keepdims=True)
    return ((xf - mu) * lax.rsqrt(var + LN_EPS)).astype(x.dtype) * g + b


def adaln(cond, w, b):
    m = jax.nn.silu(cond) @ w + b
    return m.reshape(cond.shape[0], N_MOD, D_MODEL)


def mod_slice(m, j):
    return m[:, 3 * j, None, :], m[:, 3 * j + 1, None, :], m[:, 3 * j + 2, None, :]


def ffn_sublayer(x, m, j, w_gate, w_up, w_down, g, b):
    shift, scale, gate = mod_slice(m, j)
    h = x * (1 + scale) + shift
    y = (jax.nn.silu(h @ w_gate) * (h @ w_up)) @ w_down
    return layer_norm(ALPHA * x + 0.5 * gate * y, g, b)


def mix_in(h, w_in):
    proj = h @ w_in
    q, k, v, u, vg = jnp.split(proj, [NA_DIM, 2 * NA_DIM, 3 * NA_DIM, 3 * NA_DIM + SGU_DIM], axis=-1)
    def heads(t):
        return t.reshape(t.shape[0], t.shape[1], NA_HEADS, HEAD_DIM).transpose(0, 2, 1, 3)
    return heads(q), heads(k), heads(v), jax.nn.gelu(u, approximate=False), jax.nn.gelu(vg, approximate=False)


def merge_heads(o):
    b, h, t, d = o.shape
    return o.transpose(0, 2, 1, 3).reshape(b, t, h * d)


def context_attention(q, k, v):
    b, h, s, d = q.shape
    nb = s // QUERY_BLOCK
    qb = q.reshape(b, h, nb, QUERY_BLOCK, d).transpose(2, 0, 1, 3, 4)
    scale = HEAD_DIM ** -0.5

    def block(qi):
        sc = jnp.einsum('bhqd,bhkd->bhqk', qi, k).astype(jnp.float32) * scale
        p = jax.nn.softmax(sc, axis=-1).astype(v.dtype)
        return jnp.einsum('bhqk,bhkd->bhqd', p, v)

    out = lax.map(block, qb)
    return out.transpose(1, 2, 0, 3, 4).reshape(b, h, s, d)


def neighbourhood_attention(q, k, v, k_ctx, v_ctx, rpb):
    b, h, n, d = q.shape
    rows = n // GRID_W
    wr = min(WIN_R, rows)
    r = jnp.arange(rows)
    rs = jnp.clip(r - wr // 2, 0, rows - wr)
    key_rows = rs[:, None] + jnp.arange(wr)[None, :]
    cq = jnp.arange(GRID_W)
    cs = jnp.clip(cq - WIN_C // 2, 0, GRID_W - WIN_C)
    ck = jnp.arange(GRID_W)
    col_ok = (ck[None, :] >= cs[:, None]) & (ck[None, :] < cs[:, None] + WIN_C)

    qg = q.reshape(b, h, rows, GRID_W, d)
    kb = k.reshape(b, h, rows, GRID_W, d)[:, :, key_rows]
    vb = v.reshape(b, h, rows, GRID_W, d)[:, :, key_rows]
    scale = HEAD_DIM ** -0.5

    s_loc = jnp.einsum('bhrqd,bhrikd->bhrqik', qg, kb).astype(jnp.float32) * scale
    dr = key_rows - r[:, None]
    dc = jnp.clip(ck[None, :] - cq[:, None], -(WIN_C - 1), WIN_C - 1)
    bias = rpb[:, dr[:, None, :, None] + (WIN_R - 1), dc[None, :, None, :] + (WIN_C - 1)]
    s_loc = jnp.where(col_ok[None, None, None, :, None, :], s_loc + bias.astype(jnp.float32)[None], NEG_INF)
    s_ctx = jnp.einsum('bhrqd,bhpd->bhrqp', qg, k_ctx).astype(jnp.float32) * scale

    n_loc = wr * GRID_W
    s_all = jnp.concatenate([s_loc.reshape(b, h, rows, GRID_W, n_loc), s_ctx], axis=-1)
    p = jax.nn.softmax(s_all, axis=-1).astype(v.dtype)
    p_loc = p[..., :n_loc].reshape(b, h, rows, GRID_W, wr, GRID_W)
    p_ctx = p[..., n_loc:]
    out = jnp.einsum('bhrqik,bhrikd->bhrqd', p_loc, vb) + jnp.einsum('bhrqp,bhpd->bhrqd', p_ctx, v_ctx)
    return out.reshape(b, h, n, d)


def spatial_gating(u, vg, ln_g, ln_b, w_s, b_s):
    b, t, _ = u.shape
    vn = layer_norm(vg, ln_g, ln_b).reshape(b, t // SGU_CHUNK, SGU_CHUNK, SGU_GROUPS, SGU_GC)
    s = jnp.einsum('gpq,bnqgc->bnpgc', w_s, vn) + b_s.T[None, None, :, :, None]
    return u * s.reshape(b, t, SGU_DIM)


def conv_module(h, w_in, b_in, dw, dw_b, ln_g, ln_b, w_out, b_out):
    a, gt = jnp.split(h @ w_in + b_in, 2, axis=-1)
    z = a * jax.nn.sigmoid(gt)
    z = lax.conv_general_dilated(z, dw.reshape(CONV_K, 1, CONV_DIM), window_strides=(1,),
                                 padding=[(CONV_K // 2, CONV_K // 2)],
                                 dimension_numbers=('NWC', 'WIO', 'NWC'),
                                 feature_group_count=CONV_DIM) + dw_b
    z = jax.nn.silu(layer_norm(z, ln_g, ln_b))
    return z @ w_out + b_out


def setup_inputs(seed: int = 0) -> dict:
    key = jax.random.key(seed)
    ks = jax.random.split(key, 32)
    n_even = (DEPTH + 1) // 2
    n_odd = DEPTH // 2
    f32 = jnp.float32

    def nrm(k, shape, scale):
        return jax.random.normal(k, shape, f32) * scale

    return {
        "x_prompt": nrm(ks[0], (BATCH, SEQ, D_MODEL), 1.0),
        "x_sample": nrm(ks[1], (DEC_BATCH, DEC_SEQ, D_MODEL), 1.0),
        "cache_k": nrm(ks[2], (DEC_BATCH, n_even, NA_HEADS, PAST_LEN, HEAD_DIM), 1.0),
        "cache_v": nrm(ks[3], (DEC_BATCH, n_even, NA_HEADS, PAST_LEN, HEAD_DIM), 1.0),
        "c": nrm(ks[4], (DEC_BATCH, D_MODEL), 1.0),
        "c_ctx": nrm(ks[5], (D_MODEL,), 1.0),
        "ada_w": nrm(ks[6], (DEPTH, D_MODEL, N_MOD * D_MODEL), D_MODEL ** -0.5),
        "ada_b": nrm(ks[7], (DEPTH, N_MOD * D_MODEL), 0.02),
        "ln_g": 1.0 + nrm(ks[8], (DEPTH, 3, D_MODEL), 0.02),
        "ln_b": nrm(ks[9], (DEPTH, 3, D_MODEL), 0.02),
        "ffn_w_gate": nrm(ks[10], (DEPTH, 2, D_MODEL, D_FF), D_MODEL ** -0.5),
        "ffn_w_up": nrm(ks[11], (DEPTH, 2, D_MODEL, D_FF), D_MODEL ** -0.5),
        "ffn_w_down": nrm(ks[12], (DEPTH, 2, D_FF, D_MODEL), BETA * D_FF ** -0.5),
        "mix_w_in": nrm(ks[13], (n_even, D_MODEL, MIX_IN), D_MODEL ** -0.5),
        "na_rpb": nrm(ks[14], (n_even, NA_HEADS, 2 * WIN_R - 1, 2 * WIN_C - 1), 0.1),
        "sgu_ln_g": 1.0 + nrm(ks[15], (n_even, SGU_DIM), 0.02),
        "sgu_ln_b": nrm(ks[16], (n_even, SGU_DIM), 0.02),
        "sgu_w": nrm(ks[17], (n_even, SGU_GROUPS, SGU_CHUNK, SGU_CHUNK), SGU_CHUNK ** -0.5),
        "sgu_b": 1.0 + nrm(ks[18], (n_even, SGU_GROUPS, SGU_CHUNK), 0.02),
        "mix_w_out": nrm(ks[19], (n_even, NA_DIM + SGU_DIM, D_MODEL), BETA * (NA_DIM + SGU_DIM) ** -0.5),
        "conv_w_in": nrm(ks[20], (n_odd, D_MODEL, 2 * CONV_DIM), D_MODEL ** -0.5),
        "conv_b_in": nrm(ks[21], (n_odd, 2 * CONV_DIM), 0.02),
        "conv_dw": nrm(ks[22], (n_odd, CONV_K, CONV_DIM), CONV_K ** -0.5),
        "conv_dw_b": nrm(ks[23], (n_odd, CONV_DIM), 0.02),
        "conv_ln_g": 1.0 + nrm(ks[24], (n_odd, CONV_DIM), 0.02),
        "conv_ln_b": nrm(ks[25], (n_odd, CONV_DIM), 0.02),
        "conv_w_out": nrm(ks[26], (n_odd, CONV_DIM, D_MODEL), BETA * CONV_DIM ** -0.5),
        "conv_b_out": nrm(ks[27], (n_odd, D_MODEL), 0.02),
    }


def reference(x_prompt, x_sample, cache_k, cache_v, c, c_ctx, ada_w, ada_b, ln_g, ln_b,
              ffn_w_gate, ffn_w_up, ffn_w_down, mix_w_in, na_rpb, sgu_ln_g, sgu_ln_b, sgu_w, sgu_b,
              mix_w_out, conv_w_in, conv_b_in, conv_dw, conv_dw_b, conv_ln_g, conv_ln_b,
              conv_w_out, conv_b_out):
    xp, xs = x_prompt, x_sample
    new_k, new_v = [], []
    for l in range(DEPTH):
        mp = adaln(c_ctx[None, :], ada_w[l], ada_b[l])
        ms = adaln(c, ada_w[l], ada_b[l])

        xp = ffn_sublayer(xp, mp, 0, ffn_w_gate[l, 0], ffn_w_up[l, 0], ffn_w_down[l, 0], ln_g[l, 0], ln_b[l, 0])
        xs = ffn_sublayer(xs, ms, 0, ffn_w_gate[l, 0], ffn_w_up[l, 0], ffn_w_down[l, 0], ln_g[l, 0], ln_b[l, 0])

        sh_p, sc_p, g_p = mod_slice(mp, 1)
        sh_s, sc_s, g_s = mod_slice(ms, 1)
        hp = xp * (1 + sc_p) + sh_p
        hs = xs * (1 + sc_s) + sh_s
        if l % 2 == 0:
            e = l // 2
            qp, kp, vp, up, vgp = mix_in(hp, mix_w_in[e])
            ap = merge_heads(context_attention(qp, kp, vp))
            bp = spatial_gating(up, vgp, sgu_ln_g[e], sgu_ln_b[e], sgu_w[e], sgu_b[e])
            yp = jnp.concatenate([ap, bp], axis=-1) @ mix_w_out[e]
            new_k.append(kp)
            new_v.append(vp)
            qs, ks_, vs, us, vgs = mix_in(hs, mix_w_in[e])
            as_ = merge_heads(neighbourhood_attention(qs, ks_, vs, cache_k[:, e], cache_v[:, e], na_rpb[e]))
            bs = spatial_gating(us, vgs, sgu_ln_g[e], sgu_ln_b[e], sgu_w[e], sgu_b[e])
            ys = jnp.concatenate([as_, bs], axis=-1) @ mix_w_out[e]
        else:
            o = l // 2
            yp = conv_module(hp, conv_w_in[o], conv_b_in[o], conv_dw[o], conv_dw_b[o],
                             conv_ln_g[o], conv_ln_b[o], conv_w_out[o], conv_b_out[o])
            ys = conv_module(hs, conv_w_in[o], conv_b_in[o], conv_dw[o], conv_dw_b[o],
                             conv_ln_g[o], conv_ln_b[o], conv_w_out[o], conv_b_out[o])
        xp = layer_norm(ALPHA * xp + g_p * yp, ln_g[l, 1], ln_b[l, 1])
        xs = layer_norm(ALPHA * xs + g_s * ys, ln_g[l, 1], ln_b[l, 1])

        xp = ffn_sublayer(xp, mp, 2, ffn_w_gate[l, 1], ffn_w_up[l, 1], ffn_w_down[l, 1], ln_g[l, 2], ln_b[l, 2])
        xs = ffn_sublayer(xs, ms, 2, ffn_w_gate[l, 1], ffn_w_up[l, 1], ffn_w_down[l, 1], ln_g[l, 2], ln_b[l, 2])

    new_cache_k = jnp.stack(new_k, axis=1)
    new_cache_v = jnp.stack(new_v, axis=1)
    return (xp, xs, new_cache_k, new_cache_v)
```

```python
import functools
import math

import jax
import jax.numpy as jnp
from jax import lax
from jax.experimental import pallas as pl
from jax.experimental.pallas import tpu as pltpu

F32 = jnp.float32
BF16 = jnp.bfloat16

GRID_W = 64
WIN_R = 8
WIN_C = 16
HEAD_DIM = 128
SGU_CHUNK = 128
N_MOD = 9
LN_EPS = 1e-5
NEG_INF = -1e30

V7X_VMEM_BYTES = 64 * 1024 * 1024
V7X_VMEM_LIMIT_CAP = 56 * 1024 * 1024
LANES = 128
SUBLANES = 8
MXU_COLS = 256


def _params(semantics, block_bytes, temp_bytes=0):
    need = 2 * block_bytes + temp_bytes
    limit = min(max(need + need // 4, 32 * 1024 * 1024), V7X_VMEM_LIMIT_CAP)
    return pltpu.CompilerParams(dimension_semantics=semantics, vmem_limit_bytes=int(limit))


def _nbytes(shape, dtype):
    return math.prod(shape) * jnp.dtype(dtype).itemsize


def _tile(dim, pref):
    if dim <= pref:
        return dim
    t = (pref // LANES) * LANES
    while t > LANES and dim % t:
        t -= LANES
    assert dim % t == 0, (dim, pref)
    return t


def _silu(x):
    return x * jax.nn.sigmoid(x)


def _gelu(x):
    return 0.5 * x * (1.0 + lax.erf(x * math.sqrt(0.5)))


def _layer_norm(z, g, b):
    mu = jnp.mean(z, axis=-1, keepdims=True)
    zc = z - mu
    var = jnp.mean(zc * zc, axis=-1, keepdims=True)
    return zc * lax.rsqrt(var + LN_EPS) * g + b


def _adaln_kernel(c_ref, w_ref, b_ref, o_ref):
    s = _silu(c_ref[...]).astype(BF16)
    o_ref[...] = jnp.dot(s, w_ref[...].astype(BF16), preferred_element_type=F32) + b_ref[...]


def _adaln(cond, ada_w, ada_b):
    n_layers, d, n = ada_w.shape
    rows = cond.shape[0]
    tn = _tile(n, 512)
    blocks = _nbytes((rows, d), F32) + _nbytes((d, tn), F32) + 2 * _nbytes((rows, tn), F32)
    return pl.pallas_call(
        _adaln_kernel,
        grid=(n_layers, n // tn),
        in_specs=[
            pl.BlockSpec((rows, d), lambda l, j: (0, 0)),
            pl.BlockSpec((None, d, tn), lambda l, j: (l, 0, j)),
            pl.BlockSpec((None, 1, tn), lambda l, j: (l, 0, j)),
        ],
        out_specs=pl.BlockSpec((None, rows, tn), lambda l, j: (l, 0, j)),
        out_shape=jax.ShapeDtypeStruct((n_layers, rows, n), F32),
        compiler_params=_params(("arbitrary", "arbitrary"), blocks, _nbytes((d, tn), BF16)),
        name="adaln",
    )(cond, ada_w, ada_b.reshape(n_layers, 1, n))


class _Rows:
    def __init__(self, tp, dec_seq, n_dec):
        self.tp, self.dec_seq, self.n_dec = tp, dec_seq, n_dec
        self.total = tp + dec_seq * n_dec

    def group(self, row):
        return jnp.where(row < self.tp, 0, (row - self.tp) // self.dec_seq + 1)

    def check_block(self, tr):
        assert self.tp % tr == 0 and self.dec_seq % tr == 0, (self.tp, self.dec_seq, tr)


def _mod_spec(rows, layer, tr, d):
    return pl.BlockSpec((None, None, N_MOD, d), lambda i: (layer, rows.group(i * tr), 0, 0))


def _modulate_kernel(j, x_ref, m_ref, o_ref):
    shift = m_ref[3 * j:3 * j + 1, :]
    scale = m_ref[3 * j + 1:3 * j + 2, :]
    o_ref[...] = (x_ref[...] * (1.0 + scale) + shift).astype(o_ref.dtype)


def _modulate(x, mod, rows, layer, j):
    t, d = x.shape
    tr = 256
    rows.check_block(tr)
    blocks = _nbytes((tr, d), F32) + _nbytes((tr, d), BF16) + _nbytes((N_MOD, d), F32)
    return pl.pallas_call(
        functools.partial(_modulate_kernel, j),
        grid=(t // tr,),
        in_specs=[pl.BlockSpec((tr, d), lambda i: (i, 0)), _mod_spec(rows, layer, tr, d)],
        out_specs=pl.BlockSpec((tr, d), lambda i: (i, 0)),
        out_shape=jax.ShapeDtypeStruct((t, d), BF16),
        compiler_params=_params(("arbitrary",), blocks),
        name="modulate",
    )(x, mod)


def _resid_kernel(j, weight, alpha, j_next, x_ref, y_ref, yb_ref, m_ref, g_ref, b_ref, *rest):
    gate = m_ref[3 * j + 2:3 * j + 3, :]
    z = alpha * x_ref[...] + (weight * gate) * (y_ref[...] + yb_ref[...])
    xn = _layer_norm(z, g_ref[...], b_ref[...])
    if j_next is None:
        (xo_ref,) = rest
        xo_ref[...] = xn
    else:
        mn_ref, xo_ref, ho_ref = rest
        xo_ref[...] = xn
        shift = mn_ref[3 * j_next:3 * j_next + 1, :]
        scale = mn_ref[3 * j_next + 1:3 * j_next + 2, :]
        ho_ref[...] = (xn * (1.0 + scale) + shift).astype(ho_ref.dtype)


def _resid(x, y, y_bias, mod, rows, layer, j, weight, alpha, ln_g, ln_b, nxt, row_range=None):
    t, d = x.shape
    tr = 256
    rows.check_block(tr)
    start, count = (0, t) if row_range is None else row_range
    assert start % tr == 0 and count % tr == 0
    off = start // tr
    row_spec = pl.BlockSpec((tr, d), lambda i: (i + off, 0))
    vec_spec = pl.BlockSpec((1, d), lambda i: (0, 0))

    def mod_spec(layer_):
        return pl.BlockSpec((None, None, N_MOD, d), lambda i: (layer_, rows.group((i + off) * tr), 0, 0))

    in_specs = [row_spec, row_spec, vec_spec, mod_spec(layer), vec_spec, vec_spec]
    args = [x, y, y_bias.reshape(1, d), mod, ln_g.reshape(1, d), ln_b.reshape(1, d)]
    out_spec = pl.BlockSpec((tr, d), lambda i: (i, 0))
    blocks = 3 * _nbytes((tr, d), F32) + 2 * _nbytes((N_MOD, d), F32)
    if nxt is None:
        out_specs, out_shape, j_next = out_spec, jax.ShapeDtypeStruct((count, d), F32), None
    else:
        in_specs.append(mod_spec(nxt[0]))
        args.append(mod)
        out_specs = [out_spec, out_spec]
        out_shape = [jax.ShapeDtypeStruct((count, d), F32), jax.ShapeDtypeStruct((count, d), BF16)]
        j_next = nxt[1]
        blocks += _nbytes((tr, d), BF16)
    return pl.pallas_call(
        functools.partial(_resid_kernel, j, weight, alpha, j_next),
        grid=(count // tr,),
        in_specs=in_specs,
        out_specs=out_specs,
        out_shape=out_shape,
        compiler_params=_params(("arbitrary",), blocks, 2 * _nbytes((tr, d), F32)),
        name="resid_ln",
    )(*args)


def _dual_kernel(act, has_bias, h_ref, w1_ref, w2_ref, *rest):
    h = h_ref[...]
    a = jnp.dot(h, w1_ref[...].astype(BF16), preferred_element_type=F32)
    b = jnp.dot(h, w2_ref[...].astype(BF16), preferred_element_type=F32)
    if has_bias:
        b1_ref, b2_ref, o_ref = rest
        a = a + b1_ref[...]
        b = b + b2_ref[...]
    else:
        (o_ref,) = rest
    if act == "swiglu":
        out = _silu(a) * b
    else:
        out = a * jax.nn.sigmoid(b)
    o_ref[...] = out.astype(o_ref.dtype)


def _dual(h, w1, w2, col2_off, n_cols, act, out_dtype, biases=None):
    t, k = h.shape
    tm = _tile(t, 1024)
    tn = MXU_COLS
    assert n_cols % tn == 0 and col2_off % tn == 0
    off2 = col2_off // tn
    in_specs = [
        pl.BlockSpec((tm, k), lambda m, n: (m, 0)),
        pl.BlockSpec((k, tn), lambda m, n: (0, n)),
        pl.BlockSpec((k, tn), lambda m, n: (0, n + off2)),
    ]
    args = [h, w1, w2]
    if biases is not None:
        in_specs += [pl.BlockSpec((1, tn), lambda m, n: (0, n)), pl.BlockSpec((1, tn), lambda m, n: (0, n + off2))]
        args += [biases, biases]
    blocks = _nbytes((tm, k), BF16) + 2 * _nbytes((k, tn), F32) + _nbytes((tm, tn), out_dtype)
    temps = 2 * _nbytes((k, tn), BF16) + 3 * _nbytes((tm, tn), F32)
    return pl.pallas_call(
        functools.partial(_dual_kernel, act, biases is not None),
        grid=(t // tm, n_cols // tn),
        in_specs=in_specs,
        out_specs=pl.BlockSpec((tm, tn), lambda m, n: (m, n)),
        out_shape=jax.ShapeDtypeStruct((t, n_cols), out_dtype),
        compiler_params=_params(("arbitrary", "arbitrary"), blocks, temps),
        name="dual_" + act,
    )(*args)


def _mm_kernel(n_pairs, act, *refs):
    o_ref = refs[-1]
    acc = None
    for x_ref, w_ref in zip(refs[:n_pairs], refs[n_pairs:2 * n_pairs]):
        p = jnp.dot(x_ref[...], w_ref[...].astype(BF16), preferred_element_type=F32)
        acc = p if acc is None else acc + p
    if act == "gelu":
        acc = _gelu(acc)
    o_ref[...] = acc.astype(o_ref.dtype)


def _mm(xs, w, col_off, n_cols, act, out_dtype):
    t, k = xs[0].shape
    tm = _tile(t, 1024)
    tn = _tile(math.gcd(n_cols, col_off), 512)
    assert all(x.shape == (t, k) for x in xs)
    offn = col_off // tn
    in_specs = [pl.BlockSpec((tm, k), lambda m, n: (m, 0)) for _ in xs]
    in_specs += [pl.BlockSpec((k, tn), functools.partial(lambda i, m, n: (i, n + offn), i)) for i in range(len(xs))]
    blocks = len(xs) * (_nbytes((tm, k), BF16) + _nbytes((k, tn), F32)) + _nbytes((tm, tn), out_dtype)
    temps = len(xs) * _nbytes((k, tn), BF16) + 3 * _nbytes((tm, tn), F32)
    return pl.pallas_call(
        functools.partial(_mm_kernel, len(xs), act),
        grid=(t // tm, n_cols // tn),
        in_specs=in_specs,
        out_specs=pl.BlockSpec((tm, tn), lambda m, n: (m, n)),
        out_shape=jax.ShapeDtypeStruct((t, n_cols), out_dtype),
        compiler_params=_params(("arbitrary", "arbitrary"), blocks, temps),
        name="mm_" + str(act),
    )(*xs, *([w] * len(xs)))


def _kacc_kernel(a_ref, w_ref, o_ref):
    @pl.when(pl.program_id(2) == 0)
    def _():
        o_ref[...] = jnp.zeros_like(o_ref)

    o_ref[...] += jnp.dot(a_ref[...], w_ref[...].astype(BF16), preferred_element_type=F32)


def _mm_kacc(a, w):
    t, k = a.shape
    n = w.shape[1]
    tm = _tile(t, 1024)
    tn = _tile(n, 2048)
    tk = _tile(k, 512)
    blocks = _nbytes((tm, tk), BF16) + _nbytes((tk, tn), F32) + _nbytes((tm, tn), F32)
    temps = _nbytes((tk, tn), BF16) + _nbytes((tm, tn), F32)
    return pl.pallas_call(
        _kacc_kernel,
        grid=(t // tm, n // tn, k // tk),
        in_specs=[pl.BlockSpec((tm, tk), lambda m, j, kk: (m, kk)), pl.BlockSpec((tk, tn), lambda m, j, kk: (kk, j))],
        out_specs=pl.BlockSpec((tm, tn), lambda m, j, kk: (m, j)),
        out_shape=jax.ShapeDtypeStruct((t, n), F32),
        compiler_params=_params(("arbitrary", "arbitrary", "arbitrary"), blocks, temps),
        name="mm_kacc",
    )(a, w)


def _heads_kernel(seqs, seq_len, x_ref, w_ref, o_ref):
    acc = jnp.dot(x_ref[...], w_ref[...].astype(BF16), preferred_element_type=F32)
    for s in range(seqs):
        for hh in range(MXU_COLS // HEAD_DIM):
            o_ref[s, hh] = acc[s * seq_len:(s + 1) * seq_len, hh * HEAD_DIM:(hh + 1) * HEAD_DIM]


def _mm_heads(h, w, col_off, n_cols, row_start, n_seq, seq_len):
    k = h.shape[1]
    tm = max(seq_len, _tile(n_seq * seq_len, 1024))
    assert tm % seq_len == 0 and (n_seq * seq_len) % tm == 0 and row_start % tm == 0
    seqs = tm // seq_len
    tn = MXU_COLS
    hpb = tn // HEAD_DIM
    offm, offn = row_start // tm, col_off // tn
    blocks = _nbytes((tm, k), BF16) + _nbytes((k, tn), F32) + _nbytes((tm, tn), F32)
    temps = _nbytes((k, tn), BF16) + 2 * _nbytes((tm, tn), F32)
    return pl.pallas_call(
        functools.partial(_heads_kernel, seqs, seq_len),
        grid=(n_seq // seqs, n_cols // tn),
        in_specs=[pl.BlockSpec((tm, k), lambda m, n: (m + offm, 0)), pl.BlockSpec((k, tn), lambda m, n: (0, n + offn))],
        out_specs=pl.BlockSpec((seqs, hpb, seq_len, HEAD_DIM), lambda m, n: (m, n, 0, 0)),
        out_shape=jax.ShapeDtypeStruct((n_seq, n_cols // HEAD_DIM, seq_len, HEAD_DIM), F32),
        compiler_params=_params(("arbitrary", "arbitrary"), blocks, temps),
        name="mm_heads",
    )(h, w)


_NT = (((1,), (1,)), ((), ()))


def _ctx_attn_kernel(heads, q_ref, k_ref, v_ref, o_ref):
    scale = HEAD_DIM ** -0.5
    for h in range(heads):
        q = q_ref[h].astype(BF16)
        k = k_ref[h].astype(BF16)
        v = v_ref[h].astype(BF16)
        s = lax.dot_general(q, k, _NT, preferred_element_type=F32) * scale
        p = jnp.exp(s - jnp.max(s, axis=-1, keepdims=True))
        l = jnp.sum(p, axis=-1, keepdims=True)
        o = jnp.dot(p.astype(BF16), v, preferred_element_type=F32) / l
        o_ref[:, h * HEAD_DIM:(h + 1) * HEAD_DIM] = o.astype(o_ref.dtype)


def _ctx_attention(q, k, v):
    b, nh, s, hd = q.shape
    hb = math.gcd(nh, 4)
    spec = pl.BlockSpec((None, hb, s, hd), lambda i, g: (i, g, 0, 0))
    blocks = 3 * _nbytes((hb, s, hd), F32) + _nbytes((s, hb * hd), BF16)
    return pl.pallas_call(
        functools.partial(_ctx_attn_kernel, hb),
        grid=(b, nh // hb),
        in_specs=[spec, spec, spec],
        out_specs=pl.BlockSpec((s, hb * hd), lambda i, g: (i, g)),
        out_shape=jax.ShapeDtypeStruct((b * s, nh * hd), BF16),
        compiler_params=_params(("arbitrary", "arbitrary"), blocks, 8 * _nbytes((s, s), F32)),
        name="ctx_attention",
    )(q, k, v)


def _window_start(i, n, w):
    return min(max(i - w // 2, 0), n - w)


def _na_kernel(rows, wr, q_ref, k_ref, v_ref, ck_ref, cv_ref, bias_ref, o_ref):
    scale = HEAD_DIM ** -0.5
    n_loc = wr * GRID_W
    kb = k_ref[...].astype(BF16)
    vb = v_ref[...].astype(BF16)
    ckb = ck_ref[...].astype(BF16)
    cvb = cv_ref[...].astype(BF16)
    cq = lax.broadcasted_iota(jnp.int32, (GRID_W, n_loc), 0)
    ck = lax.broadcasted_iota(jnp.int32, (GRID_W, n_loc), 1) % GRID_W
    cs = jnp.clip(cq - WIN_C // 2, 0, GRID_W - WIN_C)
    col_ok = (ck >= cs) & (ck < cs + WIN_C)
    for r in range(rows):
        rs = _window_start(r, rows, wr)
        q = q_ref[r * GRID_W:(r + 1) * GRID_W, :].astype(BF16)
        kl = kb[rs * GRID_W:rs * GRID_W + n_loc]
        vl = vb[rs * GRID_W:rs * GRID_W + n_loc]
        s_loc = lax.dot_general(q, kl, _NT, preferred_element_type=F32) * scale + bias_ref[r]
        s_loc = jnp.where(col_ok, s_loc, NEG_INF)
        s_ctx = lax.dot_general(q, ckb, _NT, preferred_element_type=F32) * scale
        m = jnp.maximum(jnp.max(s_loc, axis=-1, keepdims=True), jnp.max(s_ctx, axis=-1, keepdims=True))
        p_loc = jnp.exp(s_loc - m)
        p_ctx = jnp.exp(s_ctx - m)
        l = jnp.sum(p_loc, axis=-1, keepdims=True) + jnp.sum(p_ctx, axis=-1, keepdims=True)
        o = jnp.dot(p_loc.astype(BF16), vl, preferred_element_type=F32)
        o = o + jnp.dot(p_ctx.astype(BF16), cvb, preferred_element_type=F32)
        o_ref[r * GRID_W:(r + 1) * GRID_W, :] = (o / l).astype(o_ref.dtype)


def _na_bias(rpb, rows, wr):
    r = jnp.arange(rows)
    rs = jnp.clip(r - wr // 2, 0, rows - wr)
    dr = rs[:, None] + jnp.arange(wr)[None, :] - r[:, None] + (WIN_R - 1)
    c = jnp.arange(GRID_W)
    dc = jnp.clip(c[None, :] - c[:, None], -(WIN_C - 1), WIN_C - 1) + (WIN_C - 1)
    bias = rpb[:, dr[:, None, :, None], dc[None, :, None, :]]
    return bias.reshape(rpb.shape[0], rows, GRID_W, wr * GRID_W)


def _na_attention(q, k, v, cache_k, cache_v, layer_e, rpb):
    b, nh, n, hd = q.shape
    past = cache_k.shape[3]
    rows = n // GRID_W
    wr = min(WIN_R, rows)
    bias = _na_bias(rpb, rows, wr)
    qkv_spec = pl.BlockSpec((None, None, n, hd), lambda h, i: (i, h, 0, 0))
    ctx_spec = pl.BlockSpec((None, None, None, past, hd), lambda h, i: (i, layer_e, h, 0, 0))
    bias_spec = pl.BlockSpec((None, rows, GRID_W, wr * GRID_W), lambda h, i: (h, 0, 0, 0))
    blocks = (3 * _nbytes((n, hd), F32) + 2 * _nbytes((past, hd), F32)
              + _nbytes((rows, GRID_W, wr * GRID_W), F32) + _nbytes((n, hd), BF16))
    return pl.pallas_call(
        functools.partial(_na_kernel, rows, wr),
        grid=(nh, b),
        in_specs=[qkv_spec, qkv_spec, qkv_spec, ctx_spec, ctx_spec, bias_spec],
        out_specs=pl.BlockSpec((n, hd), lambda h, i: (i, h)),
        out_shape=jax.ShapeDtypeStruct((b * n, nh * hd), BF16),
        compiler_params=_params(("arbitrary", "arbitrary"), blocks, 4 * 1024 * 1024),
        name="na_attention",
    )(q, k, v, cache_k, cache_v, bias)


def _sgu_kernel(chunks, groups, u_ref, vg_ref, g_ref, b_ref, ws_ref, bs_ref, o_ref):
    gc = u_ref.shape[1] // groups
    for c in range(chunks):
        rows = pl.ds(c * SGU_CHUNK, SGU_CHUNK)
        vn = _layer_norm(vg_ref[rows, :], g_ref[...], b_ref[...]).astype(BF16)
        for g in range(groups):
            cols = slice(g * gc, (g + 1) * gc)
            s = jnp.dot(ws_ref[g].astype(BF16), vn[:, cols], preferred_element_type=F32) + bs_ref[g]
            o_ref[rows, cols] = (u_ref[rows, cols] * s).astype(o_ref.dtype)


def _sgu(ug, ln_g, ln_b, w_s, b_s):
    t, c2 = ug.shape
    c = c2 // 2
    groups = w_s.shape[0]
    tr = 2 * SGU_CHUNK
    assert t % tr == 0 and w_s.shape[1:] == (SGU_CHUNK, SGU_CHUNK)
    vec = pl.BlockSpec((1, c), lambda i: (0, 0))
    blocks = (2 * _nbytes((tr, c), F32) + _nbytes((tr, c), BF16) + _nbytes(w_s.shape, F32)
              + _nbytes((groups, SGU_CHUNK, LANES), F32))
    return pl.pallas_call(
        functools.partial(_sgu_kernel, tr // SGU_CHUNK, groups),
        grid=(t // tr,),
        in_specs=[
            pl.BlockSpec((tr, c), lambda i: (i, 0)),
            pl.BlockSpec((tr, c), lambda i: (i, 1)),
            vec, vec,
            pl.BlockSpec(w_s.shape, lambda i: (0, 0, 0)),
            pl.BlockSpec((groups, SGU_CHUNK, 1), lambda i: (0, 0, 0)),
        ],
        out_specs=pl.BlockSpec((tr, c), lambda i: (i, 0)),
        out_shape=jax.ShapeDtypeStruct((t, c), BF16),
        compiler_params=_params(("arbitrary",), blocks, 4 * _nbytes((SGU_CHUNK, c), F32)),
        name="sgu",
    )(ug, ug, ln_g.reshape(1, c), ln_b.reshape(1, c), w_s, b_s.reshape(groups, SGU_CHUNK, 1))


_CONV_TILE = 256
_CONV_HALO = 16
_CONV_ROWS = 64


def _conv_kernel(tp, tp_seq, dec_seq, taps, prev_ref, cur_ref, next_ref, dw_ref, dwb_ref, g_ref, b_ref, o_ref,
                 zp_ref, y_ref):
    row0 = pl.program_id(0) * _CONV_TILE
    in_ctx = row0 < tp
    seq_len = jnp.where(in_ctx, tp_seq, dec_seq)
    pos = jnp.where(in_ctx, row0, row0 - tp) % seq_len
    has_prev = pos > 0
    has_next = pos + _CONV_TILE < seq_len
    n_blk = cur_ref.shape[1] // LANES
    for cb in range(n_blk):
        cols = slice(cb * LANES, (cb + 1) * LANES)
        zp_ref[cb, 0:_CONV_HALO, :] = jnp.where(has_prev, prev_ref[:, cols], 0.0)
        zp_ref[cb, _CONV_HALO:_CONV_HALO + _CONV_TILE, :] = cur_ref[:, cols]
        zp_ref[cb, _CONV_HALO + _CONV_TILE:, :] = jnp.where(has_next, next_ref[:, cols], 0.0)
    half = taps // 2

    def col_block(cb, carry):
        for rb in range(_CONV_TILE // _CONV_ROWS):
            acc = jnp.broadcast_to(dwb_ref[cb], (_CONV_ROWS, LANES))
            for kk in range(taps):
                start = _CONV_HALO + rb * _CONV_ROWS + kk - half
                acc = acc + zp_ref[cb, pl.ds(start, _CONV_ROWS), :] * dw_ref[cb, kk:kk + 1, :]
            y_ref[cb, rb * _CONV_ROWS:(rb + 1) * _CONV_ROWS, :] = acc
        return carry

    lax.fori_loop(0, n_blk, col_block, 0)

    d = n_blk * LANES
    part = y_ref[0]
    for cb in range(1, n_blk):
        part = part + y_ref[cb]
    mu = jnp.sum(part, axis=-1, keepdims=True) / d
    part = jnp.square(y_ref[0] - mu)
    for cb in range(1, n_blk):
        part = part + jnp.square(y_ref[cb] - mu)
    rstd = lax.rsqrt(jnp.sum(part, axis=-1, keepdims=True) / d + LN_EPS)
    for cb in range(n_blk):
        cols = slice(cb * LANES, (cb + 1) * LANES)
        o_ref[:, cols] = _silu((y_ref[cb] - mu) * rstd * g_ref[:, cols] + b_ref[:, cols]).astype(o_ref.dtype)


def _conv_mid(z, dw, dw_b, ln_g, ln_b, tp, tp_seq, dec_seq):
    t, c = z.shape
    taps = dw.shape[0]
    assert taps // 2 < _CONV_HALO and tp_seq % _CONV_TILE == 0 and dec_seq % _CONV_TILE == 0 and c % LANES == 0
    n_blk = c // LANES
    hb = _CONV_TILE // _CONV_HALO
    last_halo = t // _CONV_HALO - 1
    padded = _CONV_TILE + 2 * _CONV_HALO
    vec = pl.BlockSpec((1, c), lambda i: (0, 0))
    dw_blocks = dw.reshape(taps, n_blk, LANES).transpose(1, 0, 2)
    blocks = (_nbytes((padded, c), F32) + _nbytes((n_blk, 32, LANES), F32) + _nbytes((_CONV_TILE, c), BF16))
    scratch = _nbytes((padded + _CONV_TILE, c), F32)
    return pl.pallas_call(
        functools.partial(_conv_kernel, tp, tp_seq, dec_seq, taps),
        grid=(t // _CONV_TILE,),
        in_specs=[
            pl.BlockSpec((_CONV_HALO, c), lambda i: (jnp.maximum(i * hb - 1, 0), 0)),
            pl.BlockSpec((_CONV_TILE, c), lambda i: (i, 0)),
            pl.BlockSpec((_CONV_HALO, c), lambda i: (jnp.minimum((i + 1) * hb, last_halo), 0)),
            pl.BlockSpec((n_blk, taps, LANES), lambda i: (0, 0, 0)),
            pl.BlockSpec((n_blk, 1, LANES), lambda i: (0, 0, 0)),
            vec, vec,
        ],
        out_specs=pl.BlockSpec((_CONV_TILE, c), lambda i: (i, 0)),
        out_shape=jax.ShapeDtypeStruct((t, c), BF16),
        scratch_shapes=[pltpu.VMEM((n_blk, padded, LANES), F32), pltpu.VMEM((n_blk, _CONV_TILE, LANES), F32)],
        compiler_params=_params(("arbitrary",), blocks, scratch + 2 * _nbytes((_CONV_TILE, c), F32)),
        name="conv_mid",
    )(z, z, z, dw_blocks, dw_b.reshape(n_blk, 1, LANES), ln_g.reshape(1, c), ln_b.reshape(1, c))


def kernel(x_prompt, x_sample, cache_k, cache_v, c, c_ctx, ada_w, ada_b, ln_g, ln_b, ffn_w_gate, ffn_w_up, ffn_w_down, mix_w_in, na_rpb, sgu_ln_g, sgu_ln_b, sgu_w, sgu_b, mix_w_out, conv_w_in, conv_b_in, conv_dw, conv_dw_b, conv_ln_g, conv_ln_b, conv_w_out, conv_b_out):
    batch, seq, d = x_prompt.shape
    n_dec, dec_seq, _ = x_sample.shape
    depth = ada_w.shape[0]
    n_heads = na_rpb.shape[1]
    na_dim = n_heads * HEAD_DIM
    sgu_dim = sgu_ln_g.shape[1]
    conv_dim = conv_dw.shape[2]
    alpha = (2 * depth) ** 0.25
    tp = batch * seq
    rows = _Rows(tp, dec_seq, n_dec)
    n_groups = 1 + n_dec
    assert N_MOD * d == ada_w.shape[2] and mix_w_in.shape[2] == 3 * na_dim + 2 * sgu_dim

    cond = jnp.concatenate([c_ctx[None, :], c, jnp.zeros((SUBLANES - n_groups, d), F32)], axis=0)
    mod = _adaln(cond, ada_w, ada_b)[:, :n_groups].reshape(depth, n_groups, N_MOD, d)

    x = jnp.concatenate([x_prompt.reshape(tp, d), x_sample.reshape(n_dec * dec_seq, d)], axis=0)
    h = _modulate(x, mod, rows, 0, 0)
    zero_bias = jnp.zeros((d,), F32)
    new_k, new_v = [], []
    outs = None

    for l in range(depth):
        a = _dual(h, ffn_w_gate[l, 0], ffn_w_up[l, 0], 0, ffn_w_gate.shape[3], "swiglu", BF16)
        y = _mm_kacc(a, ffn_w_down[l, 0])
        x, h = _resid(x, y, zero_bias, mod, rows, l, 0, 0.5, alpha, ln_g[l, 0], ln_b[l, 0], (l, 1))

        if l % 2 == 0:
            e = l // 2
            w_in = mix_w_in[e]
            qp, kp, vp = (_mm_heads(h, w_in, i * na_dim, na_dim, 0, batch, seq) for i in range(3))
            qs, ks, vs = (_mm_heads(h, w_in, i * na_dim, na_dim, tp, n_dec, dec_seq) for i in range(3))
            new_k.append(kp)
            new_v.append(vp)
            att = jnp.concatenate(
                [_ctx_attention(qp, kp, vp), _na_attention(qs, ks, vs, cache_k, cache_v, e, na_rpb[e])], axis=0)
            ug = _mm([h], w_in, 3 * na_dim, 2 * sgu_dim, "gelu", F32)
            gated = _sgu(ug, sgu_ln_g[e], sgu_ln_b[e], sgu_w[e], sgu_b[e])
            y = _mm([att, gated], mix_w_out[e], 0, d, None, F32)
            y_bias = zero_bias
        else:
            o = l // 2
            z = _dual(h, conv_w_in[o], conv_w_in[o], conv_dim, conv_dim, "glu", F32,
                      biases=conv_b_in[o].reshape(1, 2 * conv_dim))
            zc = _conv_mid(z, conv_dw[o], conv_dw_b[o], conv_ln_g[o], conv_ln_b[o], tp, seq, dec_seq)
            y = _mm([zc], conv_w_out[o], 0, d, None, F32)
            y_bias = conv_b_out[o]
        x, h = _resid(x, y, y_bias, mod, rows, l, 1, 1.0, alpha, ln_g[l, 1], ln_b[l, 1], (l, 2))

        a = _dual(h, ffn_w_gate[l, 1], ffn_w_up[l, 1], 0, ffn_w_gate.shape[3], "swiglu", BF16)
        y = _mm_kacc(a, ffn_w_down[l, 1])
        if l + 1 < depth:
            x, h = _resid(x, y, zero_bias, mod, rows, l, 2, 0.5, alpha, ln_g[l, 2], ln_b[l, 2], (l + 1, 0))
        else:
            args = (x, y, zero_bias, mod, rows, l, 2, 0.5, alpha, ln_g[l, 2], ln_b[l, 2], None)
            outs = (_resid(*args, row_range=(0, tp)), _resid(*args, row_range=(tp, n_dec * dec_seq)))

    y_prompt = outs[0].reshape(batch, seq, d)
    y_sample = outs[1].reshape(n_dec, dec_seq, d)
    new_cache_k = jnp.stack(new_k, axis=1)
    new_cache_v = jnp.stack(new_v, axis=1)
    return (y_prompt, y_sample, new_cache_k, new_cache_v)
```

```python
import functools
import math
from typing import NamedTuple

import jax
import jax.numpy as jnp
from jax import lax
from jax.experimental import pallas as pl
from jax.experimental.pallas import tpu as pltpu

F32 = jnp.float32
BF16 = jnp.bfloat16

GRID_W = 64
WIN_R = 8
WIN_C = 16
HEAD_DIM = 128
SGU_CHUNK = 128
N_MOD = 9
LN_EPS = 1e-5
NEG_INF = -1e30

V7X_VMEM_BYTES = 64 * 1024 * 1024
V7X_VMEM_LIMIT_CAP = 56 * 1024 * 1024
LANES = 128
SUBLANES = 8
MXU_COLS = 256


def _params(semantics, block_bytes, temp_bytes=0):
    need = 2 * block_bytes + temp_bytes
    limit = min(max(need + need // 4, 32 * 1024 * 1024), V7X_VMEM_LIMIT_CAP)
    return pltpu.CompilerParams(dimension_semantics=semantics, vmem_limit_bytes=int(limit))


def _nbytes(shape, dtype):
    return math.prod(shape) * jnp.dtype(dtype).itemsize


def _tile(dim, pref):
    if dim <= pref:
        return dim
    t = (pref // LANES) * LANES
    while t > LANES and dim % t:
        t -= LANES
    assert dim % t == 0, (dim, pref)
    return t


def _silu(x):
    return x * jax.nn.sigmoid(x)


def _gelu(x):
    return 0.5 * x * (1.0 + lax.erf(x * math.sqrt(0.5)))


def _layer_norm(z, g, b):
    mu = jnp.mean(z, axis=-1, keepdims=True)
    zc = z - mu
    var = jnp.mean(zc * zc, axis=-1, keepdims=True)
    return zc * lax.rsqrt(var + LN_EPS) * g + b


def _adaln_kernel(c_ref, w_ref, b_ref, o_ref):
    s = _silu(c_ref[...]).astype(BF16)
    o_ref[...] = jnp.dot(s, w_ref[...].astype(BF16), preferred_element_type=F32) + b_ref[...]


def _adaln(cond, ada_w, ada_b):
    n_layers, d, n = ada_w.shape
    rows = cond.shape[0]
    tn = _tile(n, 512)
    blocks = _nbytes((rows, d), F32) + _nbytes((d, tn), F32) + 2 * _nbytes((rows, tn), F32)
    return pl.pallas_call(
        _adaln_kernel,
        grid=(n_layers, n // tn),
        in_specs=[
            pl.BlockSpec((rows, d), lambda l, j: (0, 0)),
            pl.BlockSpec((None, d, tn), lambda l, j: (l, 0, j)),
            pl.BlockSpec((None, 1, tn), lambda l, j: (l, 0, j)),
        ],
        out_specs=pl.BlockSpec((None, rows, tn), lambda l, j: (l, 0, j)),
        out_shape=jax.ShapeDtypeStruct((n_layers, rows, n), F32),
        compiler_params=_params(("arbitrary", "arbitrary"), blocks, _nbytes((d, tn), BF16)),
        name="adaln",
    )(cond, ada_w, ada_b.reshape(n_layers, 1, n))


class _Rows:
    def __init__(self, tp, dec_seq, n_dec):
        self.tp, self.dec_seq, self.n_dec = tp, dec_seq, n_dec
        self.total = tp + dec_seq * n_dec

    def group(self, row):
        return jnp.where(row < self.tp, 0, (row - self.tp) // self.dec_seq + 1)

    def check_block(self, tr):
        assert self.tp % tr == 0 and self.dec_seq % tr == 0, (self.tp, self.dec_seq, tr)


def _mod_spec(rows, layer, tr, d):
    return pl.BlockSpec((None, None, N_MOD, d), lambda i: (layer, rows.group(i * tr), 0, 0))


def _modulate_kernel(j, x_ref, m_ref, o_ref):
    shift = m_ref[3 * j:3 * j + 1, :]
    scale = m_ref[3 * j + 1:3 * j + 2, :]
    o_ref[...] = (x_ref[...] * (1.0 + scale) + shift).astype(o_ref.dtype)


def _modulate(x, mod, rows, layer, j):
    t, d = x.shape
    tr = 256
    rows.check_block(tr)
    blocks = _nbytes((tr, d), F32) + _nbytes((tr, d), BF16) + _nbytes((N_MOD, d), F32)
    return pl.pallas_call(
        functools.partial(_modulate_kernel, j),
        grid=(t // tr,),
        in_specs=[pl.BlockSpec((tr, d), lambda i: (i, 0)), _mod_spec(rows, layer, tr, d)],
        out_specs=pl.BlockSpec((tr, d), lambda i: (i, 0)),
        out_shape=jax.ShapeDtypeStruct((t, d), BF16),
        compiler_params=_params(("arbitrary",), blocks),
        name="modulate",
    )(x, mod)


def _resid_kernel(j, weight, alpha, j_next, x_ref, y_ref, yb_ref, m_ref, g_ref, b_ref, *rest):
    gate = m_ref[3 * j + 2:3 * j + 3, :]
    z = alpha * x_ref[...] + (weight * gate) * (y_ref[...] + yb_ref[...])
    xn = _layer_norm(z, g_ref[...], b_ref[...])
    if j_next is None:
        (xo_ref,) = rest
        xo_ref[...] = xn
    else:
        mn_ref, xo_ref, ho_ref = rest
        xo_ref[...] = xn
        shift = mn_ref[3 * j_next:3 * j_next + 1, :]
        scale = mn_ref[3 * j_next + 1:3 * j_next + 2, :]
        ho_ref[...] = (xn * (1.0 + scale) + shift).astype(ho_ref.dtype)


def _resid(x, y, y_bias, mod, rows, layer, j, weight, alpha, ln_g, ln_b, nxt, row_range=None):
    t, d = x.shape
    tr = 256
    rows.check_block(tr)
    start, count = (0, t) if row_range is None else row_range
    assert start % tr == 0 and count % tr == 0
    off = start // tr
    row_spec = pl.BlockSpec((tr, d), lambda i: (i + off, 0))
    vec_spec = pl.BlockSpec((1, d), lambda i: (0, 0))

    def mod_spec(layer_):
        return pl.BlockSpec((None, None, N_MOD, d), lambda i: (layer_, rows.group((i + off) * tr), 0, 0))

    in_specs = [row_spec, row_spec, vec_spec, mod_spec(layer), vec_spec, vec_spec]
    args = [x, y, y_bias.reshape(1, d), mod, ln_g.reshape(1, d), ln_b.reshape(1, d)]
    out_spec = pl.BlockSpec((tr, d), lambda i: (i, 0))
    blocks = 3 * _nbytes((tr, d), F32) + 2 * _nbytes((N_MOD, d), F32)
    if nxt is None:
        out_specs, out_shape, j_next = out_spec, jax.ShapeDtypeStruct((count, d), F32), None
    else:
        in_specs.append(mod_spec(nxt[0]))
        args.append(mod)
        out_specs = [out_spec, out_spec]
        out_shape = [jax.ShapeDtypeStruct((count, d), F32), jax.ShapeDtypeStruct((count, d), BF16)]
        j_next = nxt[1]
        blocks += _nbytes((tr, d), BF16)
    return pl.pallas_call(
        functools.partial(_resid_kernel, j, weight, alpha, j_next),
        grid=(count // tr,),
        in_specs=in_specs,
        out_specs=out_specs,
        out_shape=out_shape,
        compiler_params=_params(("arbitrary",), blocks, 2 * _nbytes((tr, d), F32)),
        name="resid_ln",
    )(*args)


class _Weight(NamedTuple):
    arr: jax.Array
    lead: tuple = ()

    @property
    def shape(self):
        return self.arr.shape[len(self.lead):]

    def spec(self, block, index_fn):
        lead = self.lead
        return pl.BlockSpec((None,) * len(lead) + tuple(block), lambda *g: lead + tuple(index_fn(*g)))


def _dual_kernel(act, has_bias, n_valid, n_total, h_ref, w1_ref, w2_ref, *rest):
    o_ref = rest[-1]

    def compute():
        h = h_ref[...]
        a = jnp.dot(h, w1_ref[...].astype(BF16), preferred_element_type=F32)
        b = jnp.dot(h, w2_ref[...].astype(BF16), preferred_element_type=F32)
        if has_bias:
            a = a + rest[0][...]
            b = b + rest[1][...]
        out = _silu(a) * b if act == "swiglu" else a * jax.nn.sigmoid(b)
        o_ref[...] = out.astype(o_ref.dtype)

    if n_valid == n_total:
        compute()
    else:
        pl.when(pl.program_id(1) < n_valid)(compute)

        @pl.when(pl.program_id(1) >= n_valid)
        def _():
            o_ref[...] = jnp.zeros_like(o_ref)


def _dual(h, w1, w2, col2_off, n_cols, act, out_dtype, biases=None, pad_cols_to=None):
    t, k = h.shape
    tm = _tile(t, 2048)
    tn = MXU_COLS
    n_out = n_cols if pad_cols_to is None else pad_cols_to
    assert n_cols % tn == 0 and col2_off % tn == 0 and n_out % tn == 0 and n_out >= n_cols
    off2 = col2_off // tn
    n_valid, n_total = n_cols // tn, n_out // tn

    def col(n):
        return jnp.minimum(n, n_valid - 1)

    in_specs = [
        pl.BlockSpec((tm, k), lambda m, n: (m, 0), pipeline_mode=pl.Buffered(1)),
        w1.spec((k, tn), lambda m, n: (0, col(n))),
        w2.spec((k, tn), lambda m, n: (0, col(n) + off2)),
    ]
    args = [h, w1.arr, w2.arr]
    if biases is not None:
        in_specs += [pl.BlockSpec((1, tn), lambda m, n: (0, col(n))),
                     pl.BlockSpec((1, tn), lambda m, n: (0, col(n) + off2))]
        args += [biases, biases]
    blocks = 2 * _nbytes((k, tn), F32) + _nbytes((tm, tn), out_dtype)
    temps = _nbytes((tm, k), BF16) + 2 * _nbytes((k, tn), BF16) + 4 * _nbytes((tm, tn), F32)
    return pl.pallas_call(
        functools.partial(_dual_kernel, act, biases is not None, n_valid, n_total),
        grid=(t // tm, n_total),
        in_specs=in_specs,
        out_specs=pl.BlockSpec((tm, tn), lambda m, n: (m, n)),
        out_shape=jax.ShapeDtypeStruct((t, n_out), out_dtype),
        compiler_params=_params(("arbitrary", "arbitrary"), blocks, temps),
        name="dual_" + act,
    )(*args)


def _mm_kernel(n_pairs, act, *refs):
    o_ref = refs[-1]
    acc = None
    for x_ref, w_ref in zip(refs[:n_pairs], refs[n_pairs:2 * n_pairs]):
        p = jnp.dot(x_ref[...], w_ref[...].astype(BF16), preferred_element_type=F32)
        acc = p if acc is None else acc + p
    if act == "gelu":
        acc = _gelu(acc)
    o_ref[...] = acc.astype(o_ref.dtype)


def _mm(xs, w, col_off, n_cols, act, out_dtype):
    t, k = xs[0].shape
    tm = _tile(t, 1024)
    tn = _tile(math.gcd(n_cols, col_off), 512)
    assert all(x.shape == (t, k) for x in xs)
    offn = col_off // tn
    in_specs = [pl.BlockSpec((tm, k), lambda m, n: (m, 0)) for _ in xs]
    in_specs += [w.spec((k, tn), functools.partial(lambda i, m, n: (i, n + offn), i)) for i in range(len(xs))]
    blocks = len(xs) * (_nbytes((tm, k), BF16) + _nbytes((k, tn), F32)) + _nbytes((tm, tn), out_dtype)
    temps = len(xs) * _nbytes((k, tn), BF16) + 3 * _nbytes((tm, tn), F32)
    return pl.pallas_call(
        functools.partial(_mm_kernel, len(xs), act),
        grid=(t // tm, n_cols // tn),
        in_specs=in_specs,
        out_specs=pl.BlockSpec((tm, tn), lambda m, n: (m, n)),
        out_shape=jax.ShapeDtypeStruct((t, n_cols), out_dtype),
        compiler_params=_params(("arbitrary", "arbitrary"), blocks, temps),
        name="mm_" + str(act),
    )(*xs, *([w.arr] * len(xs)))


_KACC_SUB = 256
_KACC_NSUB = 4


def _kacc_pad(k):
    step = _KACC_SUB * _KACC_NSUB
    return -(-k // step) * step


def _kacc_kernel(a_ref, *refs):
    w_refs, o_ref = refs[:-1], refs[-1]

    @pl.when(pl.program_id(2) == 0)
    def _():
        o_ref[...] = jnp.zeros_like(o_ref)

    w = jnp.concatenate([w_ref[...].astype(BF16) for w_ref in w_refs], axis=0)
    o_ref[...] += jnp.dot(a_ref[...], w, preferred_element_type=F32)


def _mm_kacc(a, w):
    t, k_pad = a.shape
    k, n = w.shape
    assert k % _KACC_SUB == 0 and k_pad == _kacc_pad(k)
    n_sub = k // _KACC_SUB
    tm = _tile(t, 2048)
    tn = _tile(n, 1024)
    tk = _KACC_SUB * _KACC_NSUB
    w_specs = [
        w.spec((_KACC_SUB, tn), functools.partial(
            lambda s, m, j, kk: (jnp.minimum(kk * _KACC_NSUB + s, n_sub - 1), j), s))
        for s in range(_KACC_NSUB)
    ]
    blocks = _nbytes((tm, tk), BF16) + _nbytes((tk, tn), F32) + _nbytes((tm, tn), F32)
    temps = _nbytes((tk, tn), BF16) + _nbytes((tm, tn), F32)
    return pl.pallas_call(
        _kacc_kernel,
        grid=(t // tm, n // tn, k_pad // tk),
        in_specs=[pl.BlockSpec((tm, tk), lambda m, j, kk: (m, kk))] + w_specs,
        out_specs=pl.BlockSpec((tm, tn), lambda m, j, kk: (m, j)),
        out_shape=jax.ShapeDtypeStruct((t, n), F32),
        compiler_params=_params(("arbitrary", "arbitrary", "arbitrary"), blocks, temps),
        name="mm_kacc",
    )(a, *([w.arr] * _KACC_NSUB))


def _heads_kernel(seqs, seq_len, x_ref, w_ref, o_ref):
    acc = jnp.dot(x_ref[...], w_ref[...].astype(BF16), preferred_element_type=F32)
    for s in range(seqs):
        for hh in range(MXU_COLS // HEAD_DIM):
            o_ref[s, hh] = acc[s * seq_len:(s + 1) * seq_len, hh * HEAD_DIM:(hh + 1) * HEAD_DIM]


def _mm_heads(h, w, col_off, n_cols, row_start, n_seq, seq_len):
    k = h.shape[1]
    tm = max(seq_len, _tile(n_seq * seq_len, 1024))
    assert tm % seq_len == 0 and (n_seq * seq_len) % tm == 0 and row_start % tm == 0
    seqs = tm // seq_len
    tn = MXU_COLS
    hpb = tn // HEAD_DIM
    offm, offn = row_start // tm, col_off // tn
    blocks = _nbytes((tm, k), BF16) + _nbytes((k, tn), F32) + _nbytes((tm, tn), F32)
    temps = _nbytes((k, tn), BF16) + 2 * _nbytes((tm, tn), F32)
    return pl.pallas_call(
        functools.partial(_heads_kernel, seqs, seq_len),
        grid=(n_seq // seqs, n_cols // tn),
        in_specs=[pl.BlockSpec((tm, k), lambda m, n: (m + offm, 0)), w.spec((k, tn), lambda m, n: (0, n + offn))],
        out_specs=pl.BlockSpec((seqs, hpb, seq_len, HEAD_DIM), lambda m, n: (m, n, 0, 0)),
        out_shape=jax.ShapeDtypeStruct((n_seq, n_cols // HEAD_DIM, seq_len, HEAD_DIM), F32),
        compiler_params=_params(("arbitrary", "arbitrary"), blocks, temps),
        name="mm_heads",
    )(h, w.arr)


_NT = (((1,), (1,)), ((), ()))


def _ctx_attn_kernel(heads, q_ref, k_ref, v_ref, o_ref):
    scale = HEAD_DIM ** -0.5
    for h in range(heads):
        q = q_ref[h].astype(BF16)
        k = k_ref[h].astype(BF16)
        v = v_ref[h].astype(BF16)
        s = lax.dot_general(q, k, _NT, preferred_element_type=F32) * scale
        p = jnp.exp(s - jnp.max(s, axis=-1, keepdims=True))
        l = jnp.sum(p, axis=-1, keepdims=True)
        o = jnp.dot(p.astype(BF16), v, preferred_element_type=F32) / l
        o_ref[:, h * HEAD_DIM:(h + 1) * HEAD_DIM] = o.astype(o_ref.dtype)


def _ctx_attention(q, k, v):
    b, nh, s, hd = q.shape
    hb = math.gcd(nh, 4)
    spec = pl.BlockSpec((None, hb, s, hd), lambda i, g: (i, g, 0, 0))
    blocks = 3 * _nbytes((hb, s, hd), F32) + _nbytes((s, hb * hd), BF16)
    return pl.pallas_call(
        functools.partial(_ctx_attn_kernel, hb),
        grid=(b, nh // hb),
        in_specs=[spec, spec, spec],
        out_specs=pl.BlockSpec((s, hb * hd), lambda i, g: (i, g)),
        out_shape=jax.ShapeDtypeStruct((b * s, nh * hd), BF16),
        compiler_params=_params(("arbitrary", "arbitrary"), blocks, 8 * _nbytes((s, s), F32)),
        name="ctx_attention",
    )(q, k, v)


def _window_start(i, n, w):
    return min(max(i - w // 2, 0), n - w)


_N_DR = 2 * WIN_R - 1
_N_DC = 2 * WIN_C - 1
_PAIR = LANES // GRID_W


def _na_build_bias(head, rpb_ref, strips_ref):
    cq = lax.broadcasted_iota(jnp.int32, (GRID_W, LANES), 0)
    lane = lax.broadcasted_iota(jnp.int32, (GRID_W, LANES), 1)
    upper = lane >= GRID_W
    dcol = jnp.clip(lane % GRID_W - cq, -(WIN_C - 1), WIN_C - 1) + (WIN_C - 1)
    base = head * (_N_DR * _N_DC)
    for p in range(_N_DR - 1):
        def pick(j, w, p=p):
            lo = rpb_ref[base + p * _N_DC + j]
            hi = rpb_ref[base + (p + 1) * _N_DC + j]
            return jnp.where(dcol == j, jnp.where(upper, hi, lo), w)

        block = lax.fori_loop(0, _N_DC, pick, jnp.zeros((GRID_W, LANES), F32))
        strips_ref[p % _PAIR, :, (p // _PAIR) * LANES:(p // _PAIR + 1) * LANES] = block


def _na_kernel(rows, wr, rpb_ref, q_ref, k_ref, v_ref, ck_ref, cv_ref, o_ref, strips_ref):
    @pl.when(pl.program_id(1) == 0)
    def _():
        _na_build_bias(pl.program_id(0), rpb_ref, strips_ref)

    scale = HEAD_DIM ** -0.5
    n_loc = wr * GRID_W
    kb = k_ref[...].astype(BF16)
    vb = v_ref[...].astype(BF16)
    ckb = ck_ref[...].astype(BF16)
    cvb = cv_ref[...].astype(BF16)
    cq = lax.broadcasted_iota(jnp.int32, (GRID_W, n_loc), 0)
    ck = lax.broadcasted_iota(jnp.int32, (GRID_W, n_loc), 1) % GRID_W
    cs = jnp.clip(cq - WIN_C // 2, 0, GRID_W - WIN_C)
    col_ok = (ck >= cs) & (ck < cs + WIN_C)
    for r in range(rows):
        rs = _window_start(r, rows, wr)
        q = q_ref[r * GRID_W:(r + 1) * GRID_W, :].astype(BF16)
        kl = kb[rs * GRID_W:rs * GRID_W + n_loc]
        vl = vb[rs * GRID_W:rs * GRID_W + n_loc]
        d0 = rs - r + (WIN_R - 1)
        bias = strips_ref[d0 % _PAIR, :, (d0 // _PAIR) * LANES:(d0 // _PAIR) * LANES + n_loc]
        s_loc = lax.dot_general(q, kl, _NT, preferred_element_type=F32) * scale + bias
        s_loc = jnp.where(col_ok, s_loc, NEG_INF)
        s_ctx = lax.dot_general(q, ckb, _NT, preferred_element_type=F32) * scale
        m = jnp.maximum(jnp.max(s_loc, axis=-1, keepdims=True), jnp.max(s_ctx, axis=-1, keepdims=True))
        p_loc = jnp.exp(s_loc - m)
        p_ctx = jnp.exp(s_ctx - m)
        l = jnp.sum(p_loc, axis=-1, keepdims=True) + jnp.sum(p_ctx, axis=-1, keepdims=True)
        o = jnp.dot(p_loc.astype(BF16), vl, preferred_element_type=F32)
        o = o + jnp.dot(p_ctx.astype(BF16), cvb, preferred_element_type=F32)
        o_ref[r * GRID_W:(r + 1) * GRID_W, :] = (o / l).astype(o_ref.dtype)


def _na_attention(q, k, v, cache_k, cache_v, layer_e, rpb):
    b, nh, n, hd = q.shape
    past = cache_k.shape[3]
    rows = n // GRID_W
    wr = WIN_R
    assert rows >= WIN_R and WIN_R % _PAIR == 0 and rpb.shape == (nh, _N_DR, _N_DC)
    qkv_spec = pl.BlockSpec((None, None, n, hd), lambda h, i: (i, h, 0, 0))
    ctx_spec = pl.BlockSpec((None, None, None, past, hd), lambda h, i: (i, layer_e, h, 0, 0))
    strip_shape = (_PAIR, GRID_W, (_N_DR - 1 + _PAIR - 1) // _PAIR * LANES)
    blocks = 3 * _nbytes((n, hd), F32) + 2 * _nbytes((past, hd), F32) + _nbytes((n, hd), BF16)
    return pl.pallas_call(
        functools.partial(_na_kernel, rows, wr),
        grid=(nh, b),
        in_specs=[pl.BlockSpec(memory_space=pltpu.SMEM), qkv_spec, qkv_spec, qkv_spec, ctx_spec, ctx_spec],
        out_specs=pl.BlockSpec((n, hd), lambda h, i: (i, h)),
        out_shape=jax.ShapeDtypeStruct((b * n, nh * hd), BF16),
        scratch_shapes=[pltpu.VMEM(strip_shape, F32)],
        compiler_params=_params(("arbitrary", "arbitrary"), blocks, 4 * 1024 * 1024),
        name="na_attention",
    )(rpb.reshape(-1), q, k, v, cache_k, cache_v)


def _sgu_kernel(chunks, groups, u_ref, vg_ref, g_ref, b_ref, ws_ref, bs_ref, o_ref):
    gc = u_ref.shape[1] // groups
    for c in range(chunks):
        rows = pl.ds(c * SGU_CHUNK, SGU_CHUNK)
        vn = _layer_norm(vg_ref[rows, :], g_ref[...], b_ref[...]).astype(BF16)
        for g in range(groups):
            cols = slice(g * gc, (g + 1) * gc)
            s = jnp.dot(ws_ref[g].astype(BF16), vn[:, cols], preferred_element_type=F32) + bs_ref[g]
            o_ref[rows, cols] = (u_ref[rows, cols] * s).astype(o_ref.dtype)


def _sgu(ug, ln_g, ln_b, w_s, b_s):
    t, c2 = ug.shape
    c = c2 // 2
    groups = w_s.shape[0]
    tr = 2 * SGU_CHUNK
    assert t % tr == 0 and w_s.shape[1:] == (SGU_CHUNK, SGU_CHUNK)
    vec = pl.BlockSpec((1, c), lambda i: (0, 0))
    blocks = (2 * _nbytes((tr, c), F32) + _nbytes((tr, c), BF16) + _nbytes(w_s.shape, F32)
              + _nbytes((groups, SGU_CHUNK, LANES), F32))
    return pl.pallas_call(
        functools.partial(_sgu_kernel, tr // SGU_CHUNK, groups),
        grid=(t // tr,),
        in_specs=[
            pl.BlockSpec((tr, c), lambda i: (i, 0)),
            pl.BlockSpec((tr, c), lambda i: (i, 1)),
            vec, vec,
            pl.BlockSpec(w_s.shape, lambda i: (0, 0, 0)),
            pl.BlockSpec((groups, SGU_CHUNK, 1), lambda i: (0, 0, 0)),
        ],
        out_specs=pl.BlockSpec((tr, c), lambda i: (i, 0)),
        out_shape=jax.ShapeDtypeStruct((t, c), BF16),
        compiler_params=_params(("arbitrary",), blocks, 4 * _nbytes((SGU_CHUNK, c), F32)),
        name="sgu",
    )(ug, ug, ln_g.reshape(1, c), ln_b.reshape(1, c), w_s, b_s.reshape(groups, SGU_CHUNK, 1))


_CONV_TILE = 256
_CONV_HALO = 16
_CONV_ROWS = 64


def _conv_kernel(tp, tp_seq, dec_seq, taps, prev_ref, cur_ref, next_ref, dw_ref, dwb_ref, g_ref, b_ref, o_ref,
                 zp_ref, y_ref):
    row0 = pl.program_id(0) * _CONV_TILE
    in_ctx = row0 < tp
    seq_len = jnp.where(in_ctx, tp_seq, dec_seq)
    pos = jnp.where(in_ctx, row0, row0 - tp) % seq_len
    has_prev = pos > 0
    has_next = pos + _CONV_TILE < seq_len
    n_blk = cur_ref.shape[1] // LANES
    for cb in range(n_blk):
        cols = slice(cb * LANES, (cb + 1) * LANES)
        zp_ref[cb, 0:_CONV_HALO, :] = jnp.where(has_prev, prev_ref[:, cols], 0.0)
        zp_ref[cb, _CONV_HALO:_CONV_HALO + _CONV_TILE, :] = cur_ref[:, cols]
        zp_ref[cb, _CONV_HALO + _CONV_TILE:, :] = jnp.where(has_next, next_ref[:, cols], 0.0)
    half = taps // 2

    def col_block(cb, carry):
        for rb in range(_CONV_TILE // _CONV_ROWS):
            acc = jnp.broadcast_to(dwb_ref[cb], (_CONV_ROWS, LANES))
            for kk in range(taps):
                start = _CONV_HALO + rb * _CONV_ROWS + kk - half
                acc = acc + zp_ref[cb, pl.ds(start, _CONV_ROWS), :] * dw_ref[cb, kk:kk + 1, :]
            y_ref[cb, rb * _CONV_ROWS:(rb + 1) * _CONV_ROWS, :] = acc
        return carry

    lax.fori_loop(0, n_blk, col_block, 0)

    d = n_blk * LANES
    part = y_ref[0]
    for cb in range(1, n_blk):
        part = part + y_ref[cb]
    mu = jnp.sum(part, axis=-1, keepdims=True) / d
    part = jnp.square(y_ref[0] - mu)
    for cb in range(1, n_blk):
        part = part + jnp.square(y_ref[cb] - mu)
    rstd = lax.rsqrt(jnp.sum(part, axis=-1, keepdims=True) / d + LN_EPS)
    for cb in range(n_blk):
        cols = slice(cb * LANES, (cb + 1) * LANES)
        o_ref[:, cols] = _silu((y_ref[cb] - mu) * rstd * g_ref[:, cols] + b_ref[:, cols]).astype(o_ref.dtype)


def _conv_mid(z, dw, dw_b, ln_g, ln_b, tp, tp_seq, dec_seq):
    t, c = z.shape
    taps = dw.shape[0]
    assert taps // 2 < _CONV_HALO and tp_seq % _CONV_TILE == 0 and dec_seq % _CONV_TILE == 0 and c % LANES == 0
    n_blk = c // LANES
    hb = _CONV_TILE // _CONV_HALO
    last_halo = t // _CONV_HALO - 1
    padded = _CONV_TILE + 2 * _CONV_HALO
    vec = pl.BlockSpec((1, c), lambda i: (0, 0))
    dw_blocks = dw.reshape(taps, n_blk, LANES).transpose(1, 0, 2)
    blocks = (_nbytes((padded, c), F32) + _nbytes((n_blk, 32, LANES), F32) + _nbytes((_CONV_TILE, c), BF16))
    scratch = _nbytes((padded + _CONV_TILE, c), F32)
    return pl.pallas_call(
        functools.partial(_conv_kernel, tp, tp_seq, dec_seq, taps),
        grid=(t // _CONV_TILE,),
        in_specs=[
            pl.BlockSpec((_CONV_HALO, c), lambda i: (jnp.maximum(i * hb - 1, 0), 0)),
            pl.BlockSpec((_CONV_TILE, c), lambda i: (i, 0)),
            pl.BlockSpec((_CONV_HALO, c), lambda i: (jnp.minimum((i + 1) * hb, last_halo), 0)),
            pl.BlockSpec((n_blk, taps, LANES), lambda i: (0, 0, 0)),
            pl.BlockSpec((n_blk, 1, LANES), lambda i: (0, 0, 0)),
            vec, vec,
        ],
        out_specs=pl.BlockSpec((_CONV_TILE, c), lambda i: (i, 0)),
        out_shape=jax.ShapeDtypeStruct((t, c), BF16),
        scratch_shapes=[pltpu.VMEM((n_blk, padded, LANES), F32), pltpu.VMEM((n_blk, _CONV_TILE, LANES), F32)],
        compiler_params=_params(("arbitrary",), blocks, scratch + 2 * _nbytes((_CONV_TILE, c), F32)),
        name="conv_mid",
    )(z, z, z, dw_blocks, dw_b.reshape(n_blk, 1, LANES), ln_g.reshape(1, c), ln_b.reshape(1, c))


def kernel(x_prompt, x_sample, cache_k, cache_v, c, c_ctx, ada_w, ada_b, ln_g, ln_b, ffn_w_gate, ffn_w_up, ffn_w_down, mix_w_in, na_rpb, sgu_ln_g, sgu_ln_b, sgu_w, sgu_b, mix_w_out, conv_w_in, conv_b_in, conv_dw, conv_dw_b, conv_ln_g, conv_ln_b, conv_w_out, conv_b_out):
    batch, seq, d = x_prompt.shape
    n_dec, dec_seq, _ = x_sample.shape
    depth = ada_w.shape[0]
    n_heads = na_rpb.shape[1]
    na_dim = n_heads * HEAD_DIM
    sgu_dim = sgu_ln_g.shape[1]
    conv_dim = conv_dw.shape[2]
    d_ff = ffn_w_gate.shape[3]
    alpha = (2 * depth) ** 0.25
    tp = batch * seq
    rows = _Rows(tp, dec_seq, n_dec)
    n_groups = 1 + n_dec
    assert N_MOD * d == ada_w.shape[2] and mix_w_in.shape[2] == 3 * na_dim + 2 * sgu_dim

    cond = jnp.concatenate([c_ctx[None, :], c, jnp.zeros((SUBLANES - n_groups, d), F32)], axis=0)
    mod = _adaln(cond, ada_w, ada_b)[:, :n_groups].reshape(depth, n_groups, N_MOD, d)

    x = jnp.concatenate([x_prompt.reshape(tp, d), x_sample.reshape(n_dec * dec_seq, d)], axis=0)
    h = _modulate(x, mod, rows, 0, 0)
    zero_bias = jnp.zeros((d,), F32)
    new_k, new_v = [], []
    outs = None

    for l in range(depth):
        a = _dual(h, _Weight(ffn_w_gate, (l, 0)), _Weight(ffn_w_up, (l, 0)), 0, d_ff, "swiglu", BF16,
                  pad_cols_to=_kacc_pad(d_ff))
        y = _mm_kacc(a, _Weight(ffn_w_down, (l, 0)))
        x, h = _resid(x, y, zero_bias, mod, rows, l, 0, 0.5, alpha, ln_g[l, 0], ln_b[l, 0], (l, 1))

        if l % 2 == 0:
            e = l // 2
            w_in = _Weight(mix_w_in, (e,))
            qp, kp, vp = (_mm_heads(h, w_in, i * na_dim, na_dim, 0, batch, seq) for i in range(3))
            qs, ks, vs = (_mm_heads(h, w_in, i * na_dim, na_dim, tp, n_dec, dec_seq) for i in range(3))
            new_k.append(kp)
            new_v.append(vp)
            att = jnp.concatenate(
                [_ctx_attention(qp, kp, vp), _na_attention(qs, ks, vs, cache_k, cache_v, e, na_rpb[e])], axis=0)
            ug = _mm([h], w_in, 3 * na_dim, 2 * sgu_dim, "gelu", F32)
            gated = _sgu(ug, sgu_ln_g[e], sgu_ln_b[e], sgu_w[e], sgu_b[e])
            y = _mm([att, gated], _Weight(mix_w_out, (e,)), 0, d, None, F32)
            y_bias = zero_bias
        else:
            o = l // 2
            w_in = _Weight(conv_w_in, (o,))
            z = _dual(h, w_in, w_in, conv_dim, conv_dim, "glu", F32, biases=conv_b_in[o].reshape(1, 2 * conv_dim))
            zc = _conv_mid(z, conv_dw[o], conv_dw_b[o], conv_ln_g[o], conv_ln_b[o], tp, seq, dec_seq)
            y = _mm([zc], _Weight(conv_w_out, (o,)), 0, d, None, F32)
            y_bias = conv_b_out[o]
        x, h = _resid(x, y, y_bias, mod, rows, l, 1, 1.0, alpha, ln_g[l, 1], ln_b[l, 1], (l, 2))

        a = _dual(h, _Weight(ffn_w_gate, (l, 1)), _Weight(ffn_w_up, (l, 1)), 0, d_ff, "swiglu", BF16,
                  pad_cols_to=_kacc_pad(d_ff))
        y = _mm_kacc(a, _Weight(ffn_w_down, (l, 1)))
        if l + 1 < depth:
            x, h = _resid(x, y, zero_bias, mod, rows, l, 2, 0.5, alpha, ln_g[l, 2], ln_b[l, 2], (l + 1, 0))
        else:
            args = (x, y, zero_bias, mod, rows, l, 2, 0.5, alpha, ln_g[l, 2], ln_b[l, 2], None)
            outs = (_resid(*args, row_range=(0, tp)), _resid(*args, row_range=(tp, n_dec * dec_seq)))

    y_prompt = outs[0].reshape(batch, seq, d)
    y_sample = outs[1].reshape(n_dec, dec_seq, d)
    new_cache_k = jnp.stack(new_k, axis=1)
    new_cache_v = jnp.stack(new_v, axis=1)
    return (y_prompt, y_sample, new_cache_k, new_cache_v)
```

```python
import functools
import math
from typing import NamedTuple

import jax
import jax.numpy as jnp
from jax import lax
from jax.experimental import pallas as pl
from jax.experimental.pallas import tpu as pltpu

F32 = jnp.float32
BF16 = jnp.bfloat16

GRID_W = 64
WIN_R = 8
WIN_C = 16
HEAD_DIM = 128
SGU_CHUNK = 128
N_MOD = 9
LN_EPS = 1e-5
NEG_INF = -1e30

V7X_VMEM_BYTES = 64 * 1024 * 1024
V7X_VMEM_LIMIT_CAP = 56 * 1024 * 1024
LANES = 128
SUBLANES = 8
MXU_COLS = 256


def _params(semantics, block_bytes, temp_bytes=0):
    need = 2 * block_bytes + temp_bytes
    limit = min(max(need + need // 4, 32 * 1024 * 1024), V7X_VMEM_LIMIT_CAP)
    return pltpu.CompilerParams(dimension_semantics=semantics, vmem_limit_bytes=int(limit))


def _nbytes(shape, dtype):
    return math.prod(shape) * jnp.dtype(dtype).itemsize


def _tile(dim, pref):
    if dim <= pref:
        return dim
    t = (pref // LANES) * LANES
    while t > LANES and dim % t:
        t -= LANES
    assert dim % t == 0, (dim, pref)
    return t


def _silu(x):
    return x * jax.nn.sigmoid(x)


def _gelu(x):
    return 0.5 * x * (1.0 + lax.erf(x * math.sqrt(0.5)))


def _layer_norm(z, g, b):
    mu = jnp.mean(z, axis=-1, keepdims=True)
    zc = z - mu
    var = jnp.mean(zc * zc, axis=-1, keepdims=True)
    return zc * lax.rsqrt(var + LN_EPS) * g + b


def _adaln_kernel(c_ref, w_ref, b_ref, o_ref):
    s = _silu(c_ref[...]).astype(BF16)
    o_ref[...] = jnp.dot(s, w_ref[...].astype(BF16), preferred_element_type=F32) + b_ref[...]


def _adaln(cond, ada_w, ada_b):
    n_layers, d, n = ada_w.shape
    rows = cond.shape[0]
    tn = _tile(n, 1024)
    blocks = _nbytes((rows, d), F32) + _nbytes((d, tn), F32) + 2 * _nbytes((rows, tn), F32)
    return pl.pallas_call(
        _adaln_kernel,
        grid=(n_layers, n // tn),
        in_specs=[
            pl.BlockSpec((rows, d), lambda l, j: (0, 0)),
            pl.BlockSpec((None, d, tn), lambda l, j: (l, 0, j)),
            pl.BlockSpec((None, 1, tn), lambda l, j: (l, 0, j)),
        ],
        out_specs=pl.BlockSpec((None, rows, tn), lambda l, j: (l, 0, j)),
        out_shape=jax.ShapeDtypeStruct((n_layers, rows, n), F32),
        compiler_params=_params(("arbitrary", "arbitrary"), blocks, _nbytes((d, tn), BF16)),
        name="adaln",
    )(cond, ada_w, ada_b.reshape(n_layers, 1, n))


class _Rows:
    def __init__(self, tp, dec_seq, n_dec):
        self.tp, self.dec_seq, self.n_dec = tp, dec_seq, n_dec
        self.total = tp + dec_seq * n_dec

    def group(self, row):
        return jnp.where(row < self.tp, 0, (row - self.tp) // self.dec_seq + 1)

    def check_block(self, tr):
        assert self.tp % tr == 0 and self.dec_seq % tr == 0, (self.tp, self.dec_seq, tr)


def _mod_spec(rows, layer, tr, d):
    return pl.BlockSpec((None, None, N_MOD, d), lambda i: (layer, rows.group(i * tr), 0, 0))


def _stacked_specs(tr, d, n_first):
    return [pl.BlockSpec((tr, d), lambda i: (jnp.minimum(i, n_first - 1), 0)),
            pl.BlockSpec((tr, d), lambda i: (jnp.maximum(i - n_first, 0), 0))]


def _stacked_pick(n_first, first_ref, second_ref):
    return jnp.where(pl.program_id(0) < n_first, first_ref[...], second_ref[...])


def _modulate_kernel(j, n_first, xa_ref, xb_ref, m_ref, o_ref):
    shift = m_ref[3 * j:3 * j + 1, :]
    scale = m_ref[3 * j + 1:3 * j + 2, :]
    x = _stacked_pick(n_first, xa_ref, xb_ref)
    o_ref[...] = (x * (1.0 + scale) + shift).astype(o_ref.dtype)


def _modulate(xa, xb, mod, rows, layer, j):
    d = xa.shape[1]
    t = xa.shape[0] + xb.shape[0]
    tr = 256
    rows.check_block(tr)
    n_first = xa.shape[0] // tr
    blocks = 2 * _nbytes((tr, d), F32) + _nbytes((tr, d), BF16) + _nbytes((N_MOD, d), F32)
    return pl.pallas_call(
        functools.partial(_modulate_kernel, j, n_first),
        grid=(t // tr,),
        in_specs=_stacked_specs(tr, d, n_first) + [_mod_spec(rows, layer, tr, d)],
        out_specs=pl.BlockSpec((tr, d), lambda i: (i, 0)),
        out_shape=jax.ShapeDtypeStruct((t, d), BF16),
        compiler_params=_params(("arbitrary",), blocks),
        name="modulate",
    )(xa, xb, mod)


def _resid_kernel(j, weight, alpha, j_next, n_first, *refs):
    if n_first is None:
        x_ref, y_ref, yb_ref, m_ref, g_ref, b_ref, *rest = refs
        x = x_ref[...]
    else:
        xa_ref, xb_ref, y_ref, yb_ref, m_ref, g_ref, b_ref, *rest = refs
        x = _stacked_pick(n_first, xa_ref, xb_ref)
    gate = m_ref[3 * j + 2:3 * j + 3, :]
    z = alpha * x + (weight * gate) * (y_ref[...] + yb_ref[...])
    xn = _layer_norm(z, g_ref[...], b_ref[...])
    if j_next is None:
        (xo_ref,) = rest
        xo_ref[...] = xn
    else:
        mn_ref, xo_ref, ho_ref = rest
        xo_ref[...] = xn
        shift = mn_ref[3 * j_next:3 * j_next + 1, :]
        scale = mn_ref[3 * j_next + 1:3 * j_next + 2, :]
        ho_ref[...] = (xn * (1.0 + scale) + shift).astype(ho_ref.dtype)


def _resid(x, y, y_bias, mod, rows, layer, j, weight, alpha, ln_g, ln_b, nxt, row_range=None):
    t, d = y.shape
    tr = 256
    rows.check_block(tr)
    start, count = (0, t) if row_range is None else row_range
    assert start % tr == 0 and count % tr == 0
    off = start // tr
    row_spec = pl.BlockSpec((tr, d), lambda i: (i + off, 0))
    vec_spec = pl.BlockSpec((1, d), lambda i: (0, 0))

    def mod_spec(layer_):
        return pl.BlockSpec((None, None, N_MOD, d), lambda i: (layer_, rows.group((i + off) * tr), 0, 0))

    if isinstance(x, tuple):
        assert row_range is None
        n_first = x[0].shape[0] // tr
        x_specs, x_args = _stacked_specs(tr, d, n_first), list(x)
    else:
        n_first, x_specs, x_args = None, [row_spec], [x]
    in_specs = x_specs + [row_spec, vec_spec, mod_spec(layer), vec_spec, vec_spec]
    args = x_args + [y, y_bias.reshape(1, d), mod, ln_g.reshape(1, d), ln_b.reshape(1, d)]
    out_spec = pl.BlockSpec((tr, d), lambda i: (i, 0))
    blocks = (2 + len(x_args)) * _nbytes((tr, d), F32) + 2 * _nbytes((N_MOD, d), F32)
    if nxt is None:
        out_specs, out_shape, j_next = out_spec, jax.ShapeDtypeStruct((count, d), F32), None
    else:
        in_specs.append(mod_spec(nxt[0]))
        args.append(mod)
        out_specs = [out_spec, out_spec]
        out_shape = [jax.ShapeDtypeStruct((count, d), F32), jax.ShapeDtypeStruct((count, d), BF16)]
        j_next = nxt[1]
        blocks += _nbytes((tr, d), BF16)
    return pl.pallas_call(
        functools.partial(_resid_kernel, j, weight, alpha, j_next, n_first),
        grid=(count // tr,),
        in_specs=in_specs,
        out_specs=out_specs,
        out_shape=out_shape,
        compiler_params=_params(("arbitrary",), blocks, 2 * _nbytes((tr, d), F32)),
        name="resid_ln",
    )(*args)


class _Weight(NamedTuple):
    arr: jax.Array
    lead: tuple = ()

    @property
    def shape(self):
        return self.arr.shape[len(self.lead):]

    def spec(self, block, index_fn):
        lead = self.lead
        return pl.BlockSpec((None,) * len(lead) + tuple(block), lambda *g: lead + tuple(index_fn(*g)))


def _dual_kernel(act, has_bias, n_valid, n_total, h_ref, w1_ref, w2_ref, *rest):
    o_ref = rest[-1]

    def compute():
        h = h_ref[...]
        a = jnp.dot(h, w1_ref[...].astype(BF16), preferred_element_type=F32)
        b = jnp.dot(h, w2_ref[...].astype(BF16), preferred_element_type=F32)
        if has_bias:
            a = a + rest[0][...]
            b = b + rest[1][...]
        out = _silu(a) * b if act == "swiglu" else a * jax.nn.sigmoid(b)
        o_ref[...] = out.astype(o_ref.dtype)

    if n_valid == n_total:
        compute()
    else:
        pl.when(pl.program_id(1) < n_valid)(compute)

        @pl.when(pl.program_id(1) >= n_valid)
        def _():
            o_ref[...] = jnp.zeros_like(o_ref)


def _dual(h, w1, w2, col2_off, n_cols, act, out_dtype, biases=None, pad_cols_to=None):
    t, k = h.shape
    tm = _tile(t, 2048)
    tn = MXU_COLS
    n_out = n_cols if pad_cols_to is None else pad_cols_to
    assert n_cols % tn == 0 and col2_off % tn == 0 and n_out % tn == 0 and n_out >= n_cols
    off2 = col2_off // tn
    n_valid, n_total = n_cols // tn, n_out // tn

    def col(n):
        return jnp.minimum(n, n_valid - 1)

    in_specs = [
        pl.BlockSpec((tm, k), lambda m, n: (m, 0), pipeline_mode=pl.Buffered(1)),
        w1.spec((k, tn), lambda m, n: (0, col(n))),
        w2.spec((k, tn), lambda m, n: (0, col(n) + off2)),
    ]
    args = [h, w1.arr, w2.arr]
    if biases is not None:
        in_specs += [pl.BlockSpec((1, tn), lambda m, n: (0, col(n))),
                     pl.BlockSpec((1, tn), lambda m, n: (0, col(n) + off2))]
        args += [biases, biases]
    blocks = 2 * _nbytes((k, tn), F32) + _nbytes((tm, tn), out_dtype)
    temps = _nbytes((tm, k), BF16) + 2 * _nbytes((k, tn), BF16) + 4 * _nbytes((tm, tn), F32)
    return pl.pallas_call(
        functools.partial(_dual_kernel, act, biases is not None, n_valid, n_total),
        grid=(t // tm, n_total),
        in_specs=in_specs,
        out_specs=pl.BlockSpec((tm, tn), lambda m, n: (m, n)),
        out_shape=jax.ShapeDtypeStruct((t, n_out), out_dtype),
        compiler_params=_params(("arbitrary", "arbitrary"), blocks, temps),
        name="dual_" + act,
    )(*args)


def _mm_kernel(n_pairs, act, *refs):
    o_ref = refs[-1]
    acc = None
    for x_ref, w_ref in zip(refs[:n_pairs], refs[n_pairs:2 * n_pairs]):
        p = jnp.dot(x_ref[...], w_ref[...].astype(BF16), preferred_element_type=F32)
        acc = p if acc is None else acc + p
    if act == "gelu":
        acc = _gelu(acc)
    o_ref[...] = acc.astype(o_ref.dtype)


def _mm(xs, w, col_off, n_cols, act, out_dtype):
    t, k = xs[0].shape
    tm = _tile(t, 1024)
    tn = _tile(math.gcd(n_cols, col_off), 512)
    assert all(x.shape == (t, k) for x in xs)
    offn = col_off // tn
    in_specs = [pl.BlockSpec((tm, k), lambda m, n: (m, 0)) for _ in xs]
    in_specs += [w.spec((k, tn), functools.partial(lambda i, m, n: (i, n + offn), i)) for i in range(len(xs))]
    blocks = len(xs) * (_nbytes((tm, k), BF16) + _nbytes((k, tn), F32)) + _nbytes((tm, tn), out_dtype)
    temps = len(xs) * _nbytes((k, tn), BF16) + 3 * _nbytes((tm, tn), F32)
    return pl.pallas_call(
        functools.partial(_mm_kernel, len(xs), act),
        grid=(t // tm, n_cols // tn),
        in_specs=in_specs,
        out_specs=pl.BlockSpec((tm, tn), lambda m, n: (m, n)),
        out_shape=jax.ShapeDtypeStruct((t, n_cols), out_dtype),
        compiler_params=_params(("arbitrary", "arbitrary"), blocks, temps),
        name="mm_" + str(act),
    )(*xs, *([w.arr] * len(xs)))


_KACC_SUB = 256
_KACC_NSUB = 4


def _kacc_pad(k):
    step = _KACC_SUB * _KACC_NSUB
    return -(-k // step) * step


def _kacc_kernel(a_ref, *refs):
    w_refs, o_ref = refs[:-1], refs[-1]

    @pl.when(pl.program_id(2) == 0)
    def _():
        o_ref[...] = jnp.zeros_like(o_ref)

    w = jnp.concatenate([w_ref[...].astype(BF16) for w_ref in w_refs], axis=0)
    o_ref[...] += jnp.dot(a_ref[...], w, preferred_element_type=F32)


def _mm_kacc(a, w):
    t, k_pad = a.shape
    k, n = w.shape
    assert k % _KACC_SUB == 0 and k_pad == _kacc_pad(k)
    n_sub = k // _KACC_SUB
    tm = _tile(t, 2048)
    tn = _tile(n, 1024)
    tk = _KACC_SUB * _KACC_NSUB
    w_specs = [
        w.spec((_KACC_SUB, tn), functools.partial(
            lambda s, m, j, kk: (jnp.minimum(kk * _KACC_NSUB + s, n_sub - 1), j), s))
        for s in range(_KACC_NSUB)
    ]
    blocks = _nbytes((tm, tk), BF16) + _nbytes((tk, tn), F32) + _nbytes((tm, tn), F32)
    temps = _nbytes((tk, tn), BF16) + _nbytes((tm, tn), F32)
    return pl.pallas_call(
        _kacc_kernel,
        grid=(t // tm, n // tn, k_pad // tk),
        in_specs=[pl.BlockSpec((tm, tk), lambda m, j, kk: (m, kk))] + w_specs,
        out_specs=pl.BlockSpec((tm, tn), lambda m, j, kk: (m, j)),
        out_shape=jax.ShapeDtypeStruct((t, n), F32),
        compiler_params=_params(("arbitrary", "arbitrary", "arbitrary"), blocks, temps),
        name="mm_kacc",
    )(a, *([w.arr] * _KACC_NSUB))


def _heads_kernel(seqs, seq_len, x_ref, w_ref, o_ref):
    acc = jnp.dot(x_ref[...], w_ref[...].astype(BF16), preferred_element_type=F32)
    for s in range(seqs):
        for hh in range(o_ref.shape[1]):
            o_ref[s, hh] = acc[s * seq_len:(s + 1) * seq_len, hh * HEAD_DIM:(hh + 1) * HEAD_DIM]


def _mm_heads(h, w, col_off, n_cols, row_start, n_seq, seq_len):
    k = h.shape[1]
    tm = max(seq_len, _tile(n_seq * seq_len, 1024))
    assert tm % seq_len == 0 and (n_seq * seq_len) % tm == 0 and row_start % tm == 0
    seqs = tm // seq_len
    tn = _tile(math.gcd(n_cols, col_off), 2 * MXU_COLS)
    hpb = tn // HEAD_DIM
    offm, offn = row_start // tm, col_off // tn
    blocks = _nbytes((tm, k), BF16) + _nbytes((k, tn), F32) + _nbytes((tm, tn), F32)
    temps = _nbytes((k, tn), BF16) + 2 * _nbytes((tm, tn), F32)
    return pl.pallas_call(
        functools.partial(_heads_kernel, seqs, seq_len),
        grid=(n_seq // seqs, n_cols // tn),
        in_specs=[pl.BlockSpec((tm, k), lambda m, n: (m + offm, 0)), w.spec((k, tn), lambda m, n: (0, n + offn))],
        out_specs=pl.BlockSpec((seqs, hpb, seq_len, HEAD_DIM), lambda m, n: (m, n, 0, 0)),
        out_shape=jax.ShapeDtypeStruct((n_seq, n_cols // HEAD_DIM, seq_len, HEAD_DIM), F32),
        compiler_params=_params(("arbitrary", "arbitrary"), blocks, temps),
        name="mm_heads",
    )(h, w.arr)


_NT = (((1,), (1,)), ((), ()))


def _ctx_attn_kernel(heads, q_ref, k_ref, v_ref, o_ref):
    scale = HEAD_DIM ** -0.5
    for h in range(heads):
        q = q_ref[h].astype(BF16)
        k = k_ref[h].astype(BF16)
        v = v_ref[h].astype(BF16)
        s = lax.dot_general(q, k, _NT, preferred_element_type=F32) * scale
        p = jnp.exp(s - jnp.max(s, axis=-1, keepdims=True))
        l = jnp.sum(p, axis=-1, keepdims=True)
        o = jnp.dot(p.astype(BF16), v, preferred_element_type=F32) / l
        o_ref[:, h * HEAD_DIM:(h + 1) * HEAD_DIM] = o.astype(o_ref.dtype)


def _ctx_attention(q, k, v):
    b, nh, s, hd = q.shape
    hb = math.gcd(nh, 4)
    spec = pl.BlockSpec((None, hb, s, hd), lambda i, g: (i, g, 0, 0))
    blocks = 3 * _nbytes((hb, s, hd), F32) + _nbytes((s, hb * hd), BF16)
    return pl.pallas_call(
        functools.partial(_ctx_attn_kernel, hb),
        grid=(b, nh // hb),
        in_specs=[spec, spec, spec],
        out_specs=pl.BlockSpec((s, hb * hd), lambda i, g: (i, g)),
        out_shape=jax.ShapeDtypeStruct((b * s, nh * hd), BF16),
        compiler_params=_params(("arbitrary", "arbitrary"), blocks, 8 * _nbytes((s, s), F32)),
        name="ctx_attention",
    )(q, k, v)


def _window_start(i, n, w):
    return min(max(i - w // 2, 0), n - w)


_N_DR = 2 * WIN_R - 1
_N_DC = 2 * WIN_C - 1
_PAIR = LANES // GRID_W


_NA_GROUP = 4
_NA_UNION = 12


class _NaPlan(NamedTuple):
    groups: tuple
    pad_left: int
    n_blocks: int


def _na_plan(rows):
    assert rows % _NA_GROUP == 0 and rows >= _NA_UNION and GRID_W * _PAIR == LANES and _NA_UNION % _PAIR == 0
    groups = []
    for r0 in range(0, rows, _NA_GROUP):
        base = min(_window_start(r0, rows, WIN_R), rows - _NA_UNION)
        bands = []
        for r in range(r0, r0 + _NA_GROUP):
            rs = _window_start(r, rows, WIN_R)
            assert base <= rs and rs + WIN_R <= base + _NA_UNION
            bands.append((rs - base, base - r + WIN_R - 1))
        groups.append((r0, base, tuple(bands)))
    firsts = [first for _, _, bands in groups for _, first in bands]
    pad_left = max(0, -min(firsts))
    pad_right = max(0, max(firsts) + _NA_UNION - _N_DR)
    return _NaPlan(tuple(groups), pad_left, pad_left + _N_DR + pad_right)


def _na_strip_slots(plan):
    return (plan.n_blocks - 1 + _PAIR - 1) // _PAIR


def _na_build_bias(head, plan, rpb_ref, strips_ref, bias_ref):
    lane8 = lax.broadcasted_iota(jnp.int32, (SUBLANES, LANES), 1)
    tap = jnp.clip(lane8 - (GRID_W - 1), -(WIN_C - 1), WIN_C - 1) + (WIN_C - 1)
    table = head * (_N_DR * _N_DC)
    lower, upper = [], []
    for d in range(_N_DR):
        def pick(j, e, d=d):
            return jnp.where(tap == j, rpb_ref[table + d * _N_DC + j], e)

        e = lax.fori_loop(0, _N_DC, pick, jnp.zeros((SUBLANES, LANES), F32))
        e = jnp.broadcast_to(e[0:1, :], (GRID_W, LANES))
        lower.append(pltpu.roll(e, LANES - (GRID_W - 1), 1, stride=1, stride_axis=0))
        upper.append(pltpu.roll(e, 1, 1, stride=1, stride_axis=0))
    zero = jnp.zeros((GRID_W, LANES), F32)
    in_upper = lax.broadcasted_iota(jnp.int32, (GRID_W, LANES), 1) >= GRID_W

    def block(b, halves):
        d = b - plan.pad_left
        return halves[d] if 0 <= d < _N_DR else zero

    for q in range(plan.n_blocks - 1):
        strips_ref[q % _PAIR, :, (q // _PAIR) * LANES:(q // _PAIR + 1) * LANES] = jnp.where(
            in_upper, block(q + 1, upper), block(q, lower))

    shape = (_NA_GROUP * GRID_W, _NA_UNION * GRID_W)
    row = lax.broadcasted_iota(jnp.int32, shape, 0)
    col = lax.broadcasted_iota(jnp.int32, shape, 1)
    band, cq = row // GRID_W, row % GRID_W
    urow, ck = col // GRID_W, col % GRID_W
    cs = jnp.clip(cq - WIN_C // 2, 0, GRID_W - WIN_C)
    col_ok = (ck >= cs) & (ck < cs + WIN_C)
    for g, (_, _, bands) in enumerate(plan.groups):
        lo = jnp.zeros(shape, jnp.int32)
        pieces = []
        for j, (lo_j, first) in enumerate(bands):
            lo = jnp.where(band == j, lo_j, lo)
            s = first + plan.pad_left
            pieces.append(strips_ref[s % _PAIR, :, (s // _PAIR) * LANES:(s // _PAIR) * LANES + shape[1]])
        ok = col_ok & (urow >= lo) & (urow < lo + WIN_R)
        bias_ref[g] = jnp.where(ok, jnp.concatenate(pieces, axis=0), NEG_INF)


def _na_kernel(plan, rpb_ref, q_ref, k_ref, v_ref, ck_ref, cv_ref, o_ref, strips_ref, bias_ref):
    @pl.when(pl.program_id(1) == 0)
    def _():
        _na_build_bias(pl.program_id(0), plan, rpb_ref, strips_ref, bias_ref)

    scale = HEAD_DIM ** -0.5
    kb = k_ref[...].astype(BF16)
    vb = v_ref[...].astype(BF16)
    ckb = ck_ref[...].astype(BF16)
    cvb = cv_ref[...].astype(BF16)
    for g, (r0, base, _) in enumerate(plan.groups):
        queries = slice(r0 * GRID_W, (r0 + _NA_GROUP) * GRID_W)
        keys = slice(base * GRID_W, (base + _NA_UNION) * GRID_W)
        q = q_ref[queries, :].astype(BF16)
        s_loc = lax.dot_general(q, kb[keys], _NT, preferred_element_type=F32) * scale + bias_ref[g]
        s_ctx = lax.dot_general(q, ckb, _NT, preferred_element_type=F32) * scale
        m = jnp.maximum(jnp.max(s_loc, axis=-1, keepdims=True), jnp.max(s_ctx, axis=-1, keepdims=True))
        p_loc = jnp.exp(s_loc - m)
        p_ctx = jnp.exp(s_ctx - m)
        l = jnp.sum(p_loc, axis=-1, keepdims=True) + jnp.sum(p_ctx, axis=-1, keepdims=True)
        o = jnp.dot(p_loc.astype(BF16), vb[keys], preferred_element_type=F32)
        o = o + jnp.dot(p_ctx.astype(BF16), cvb, preferred_element_type=F32)
        o_ref[queries, :] = (o / l).astype(o_ref.dtype)


def _na_attention(q, k, v, cache_k, cache_v, layer_e, rpb):
    b, nh, n, hd = q.shape
    past = cache_k.shape[3]
    plan = _na_plan(n // GRID_W)
    assert rpb.shape == (nh, _N_DR, _N_DC)
    qkv_spec = pl.BlockSpec((None, None, n, hd), lambda h, i: (i, h, 0, 0))
    ctx_spec = pl.BlockSpec((None, None, None, past, hd), lambda h, i: (i, layer_e, h, 0, 0))
    strip_shape = (_PAIR, GRID_W, _na_strip_slots(plan) * LANES)
    bias_shape = (len(plan.groups), _NA_GROUP * GRID_W, _NA_UNION * GRID_W)
    blocks = 3 * _nbytes((n, hd), F32) + 2 * _nbytes((past, hd), F32) + _nbytes((n, hd), BF16)
    scratch = _nbytes(strip_shape, F32) + _nbytes(bias_shape, F32)
    return pl.pallas_call(
        functools.partial(_na_kernel, plan),
        grid=(nh, b),
        in_specs=[pl.BlockSpec(memory_space=pltpu.SMEM), qkv_spec, qkv_spec, qkv_spec, ctx_spec, ctx_spec],
        out_specs=pl.BlockSpec((n, hd), lambda h, i: (i, h)),
        out_shape=jax.ShapeDtypeStruct((b * n, nh * hd), BF16),
        scratch_shapes=[pltpu.VMEM(strip_shape, F32), pltpu.VMEM(bias_shape, F32)],
        compiler_params=_params(("arbitrary", "arbitrary"), blocks, scratch + 8 * _nbytes(bias_shape[1:], F32)),
        name="na_attention",
    )(rpb.reshape(-1), q, k, v, cache_k, cache_v)


def _sgu_kernel(chunks, groups, u_ref, vg_ref, g_ref, b_ref, ws_ref, bs_ref, o_ref):
    gc = u_ref.shape[1] // groups
    for c in range(chunks):
        rows = pl.ds(c * SGU_CHUNK, SGU_CHUNK)
        vn = _layer_norm(vg_ref[rows, :], g_ref[...], b_ref[...]).astype(BF16)
        for g in range(groups):
            cols = slice(g * gc, (g + 1) * gc)
            s = jnp.dot(ws_ref[g].astype(BF16), vn[:, cols], preferred_element_type=F32) + bs_ref[g]
            o_ref[rows, cols] = (u_ref[rows, cols] * s).astype(o_ref.dtype)


def _sgu(ug, ln_g, ln_b, w_s, b_s):
    t, c2 = ug.shape
    c = c2 // 2
    groups = w_s.shape[0]
    tr = 2 * SGU_CHUNK
    assert t % tr == 0 and w_s.shape[1:] == (SGU_CHUNK, SGU_CHUNK)
    vec = pl.BlockSpec((1, c), lambda i: (0, 0))
    blocks = (2 * _nbytes((tr, c), F32) + _nbytes((tr, c), BF16) + _nbytes(w_s.shape, F32)
              + _nbytes((groups, SGU_CHUNK, LANES), F32))
    return pl.pallas_call(
        functools.partial(_sgu_kernel, tr // SGU_CHUNK, groups),
        grid=(t // tr,),
        in_specs=[
            pl.BlockSpec((tr, c), lambda i: (i, 0)),
            pl.BlockSpec((tr, c), lambda i: (i, 1)),
            vec, vec,
            pl.BlockSpec(w_s.shape, lambda i: (0, 0, 0)),
            pl.BlockSpec((groups, SGU_CHUNK, 1), lambda i: (0, 0, 0)),
        ],
        out_specs=pl.BlockSpec((tr, c), lambda i: (i, 0)),
        out_shape=jax.ShapeDtypeStruct((t, c), BF16),
        compiler_params=_params(("arbitrary",), blocks, 4 * _nbytes((SGU_CHUNK, c), F32)),
        name="sgu",
    )(ug, ug, ln_g.reshape(1, c), ln_b.reshape(1, c), w_s, b_s.reshape(groups, SGU_CHUNK, 1))


_CONV_TILE = 256
_CONV_HALO = 16
_CONV_ROWS = 64


def _conv_kernel(tp, tp_seq, dec_seq, taps, prev_ref, cur_ref, next_ref, dw_ref, dwb_ref, g_ref, b_ref, o_ref,
                 zp_ref, y_ref):
    row0 = pl.program_id(0) * _CONV_TILE
    in_ctx = row0 < tp
    seq_len = jnp.where(in_ctx, tp_seq, dec_seq)
    pos = jnp.where(in_ctx, row0, row0 - tp) % seq_len
    has_prev = pos > 0
    has_next = pos + _CONV_TILE < seq_len
    n_blk = cur_ref.shape[1] // LANES
    for cb in range(n_blk):
        cols = slice(cb * LANES, (cb + 1) * LANES)
        zp_ref[cb, 0:_CONV_HALO, :] = jnp.where(has_prev, prev_ref[:, cols], 0.0)
        zp_ref[cb, _CONV_HALO:_CONV_HALO + _CONV_TILE, :] = cur_ref[:, cols]
        zp_ref[cb, _CONV_HALO + _CONV_TILE:, :] = jnp.where(has_next, next_ref[:, cols], 0.0)
    half = taps // 2

    def col_block(cb, carry):
        for rb in range(_CONV_TILE // _CONV_ROWS):
            acc = jnp.broadcast_to(dwb_ref[cb], (_CONV_ROWS, LANES))
            for kk in range(taps):
                start = _CONV_HALO + rb * _CONV_ROWS + kk - half
                acc = acc + zp_ref[cb, pl.ds(start, _CONV_ROWS), :] * dw_ref[cb, kk:kk + 1, :]
            y_ref[cb, rb * _CONV_ROWS:(rb + 1) * _CONV_ROWS, :] = acc
        return carry

    lax.fori_loop(0, n_blk, col_block, 0)

    d = n_blk * LANES
    part = y_ref[0]
    for cb in range(1, n_blk):
        part = part + y_ref[cb]
    mu = jnp.sum(part, axis=-1, keepdims=True) / d
    part = jnp.square(y_ref[0] - mu)
    for cb in range(1, n_blk):
        part = part + jnp.square(y_ref[cb] - mu)
    rstd = lax.rsqrt(jnp.sum(part, axis=-1, keepdims=True) / d + LN_EPS)
    for cb in range(n_blk):
        cols = slice(cb * LANES, (cb + 1) * LANES)
        o_ref[:, cols] = _silu((y_ref[cb] - mu) * rstd * g_ref[:, cols] + b_ref[:, cols]).astype(o_ref.dtype)


def _conv_mid(z, dw, dw_b, ln_g, ln_b, tp, tp_seq, dec_seq):
    t, c = z.shape
    taps = dw.shape[0]
    assert taps // 2 < _CONV_HALO and tp_seq % _CONV_TILE == 0 and dec_seq % _CONV_TILE == 0 and c % LANES == 0
    n_blk = c // LANES
    hb = _CONV_TILE // _CONV_HALO
    last_halo = t // _CONV_HALO - 1
    padded = _CONV_TILE + 2 * _CONV_HALO
    vec = pl.BlockSpec((1, c), lambda i: (0, 0))
    dw_blocks = dw.reshape(taps, n_blk, LANES).transpose(1, 0, 2)
    blocks = (_nbytes((padded, c), F32) + _nbytes((n_blk, 32, LANES), F32) + _nbytes((_CONV_TILE, c), BF16))
    scratch = _nbytes((padded + _CONV_TILE, c), F32)
    return pl.pallas_call(
        functools.partial(_conv_kernel, tp, tp_seq, dec_seq, taps),
        grid=(t // _CONV_TILE,),
        in_specs=[
            pl.BlockSpec((_CONV_HALO, c), lambda i: (jnp.maximum(i * hb - 1, 0), 0)),
            pl.BlockSpec((_CONV_TILE, c), lambda i: (i, 0)),
            pl.BlockSpec((_CONV_HALO, c), lambda i: (jnp.minimum((i + 1) * hb, last_halo), 0)),
            pl.BlockSpec((n_blk, taps, LANES), lambda i: (0, 0, 0)),
            pl.BlockSpec((n_blk, 1, LANES), lambda i: (0, 0, 0)),
            vec, vec,
        ],
        out_specs=pl.BlockSpec((_CONV_TILE, c), lambda i: (i, 0)),
        out_shape=jax.ShapeDtypeStruct((t, c), BF16),
        scratch_shapes=[pltpu.VMEM((n_blk, padded, LANES), F32), pltpu.VMEM((n_blk, _CONV_TILE, LANES), F32)],
        compiler_params=_params(("arbitrary",), blocks, scratch + 2 * _nbytes((_CONV_TILE, c), F32)),
        name="conv_mid",
    )(z, z, z, dw_blocks, dw_b.reshape(n_blk, 1, LANES), ln_g.reshape(1, c), ln_b.reshape(1, c))


def kernel(x_prompt, x_sample, cache_k, cache_v, c, c_ctx, ada_w, ada_b, ln_g, ln_b, ffn_w_gate, ffn_w_up, ffn_w_down, mix_w_in, na_rpb, sgu_ln_g, sgu_ln_b, sgu_w, sgu_b, mix_w_out, conv_w_in, conv_b_in, conv_dw, conv_dw_b, conv_ln_g, conv_ln_b, conv_w_out, conv_b_out):
    batch, seq, d = x_prompt.shape
    n_dec, dec_seq, _ = x_sample.shape
    depth = ada_w.shape[0]
    n_heads = na_rpb.shape[1]
    na_dim = n_heads * HEAD_DIM
    sgu_dim = sgu_ln_g.shape[1]
    conv_dim = conv_dw.shape[2]
    d_ff = ffn_w_gate.shape[3]
    alpha = (2 * depth) ** 0.25
    tp = batch * seq
    rows = _Rows(tp, dec_seq, n_dec)
    n_groups = 1 + n_dec
    assert N_MOD * d == ada_w.shape[2] and mix_w_in.shape[2] == 3 * na_dim + 2 * sgu_dim

    cond = jnp.concatenate([c_ctx[None, :], c, jnp.zeros((SUBLANES - n_groups, d), F32)], axis=0)
    mod = _adaln(cond, ada_w, ada_b)[:, :n_groups].reshape(depth, n_groups, N_MOD, d)

    x = (x_prompt.reshape(tp, d), x_sample.reshape(n_dec * dec_seq, d))
    h = _modulate(x[0], x[1], mod, rows, 0, 0)
    zero_bias = jnp.zeros((d,), F32)
    new_k, new_v = [], []
    outs = None

    for l in range(depth):
        a = _dual(h, _Weight(ffn_w_gate, (l, 0)), _Weight(ffn_w_up, (l, 0)), 0, d_ff, "swiglu", BF16,
                  pad_cols_to=_kacc_pad(d_ff))
        y = _mm_kacc(a, _Weight(ffn_w_down, (l, 0)))
        x, h = _resid(x, y, zero_bias, mod, rows, l, 0, 0.5, alpha, ln_g[l, 0], ln_b[l, 0], (l, 1))

        if l % 2 == 0:
            e = l // 2
            w_in = _Weight(mix_w_in, (e,))
            qp, kp, vp = (_mm_heads(h, w_in, i * na_dim, na_dim, 0, batch, seq) for i in range(3))
            qs, ks, vs = (_mm_heads(h, w_in, i * na_dim, na_dim, tp, n_dec, dec_seq) for i in range(3))
            new_k.append(kp)
            new_v.append(vp)
            att = jnp.concatenate(
                [_ctx_attention(qp, kp, vp), _na_attention(qs, ks, vs, cache_k, cache_v, e, na_rpb[e])], axis=0)
            ug = _mm([h], w_in, 3 * na_dim, 2 * sgu_dim, "gelu", F32)
            gated = _sgu(ug, sgu_ln_g[e], sgu_ln_b[e], sgu_w[e], sgu_b[e])
            y = _mm([att, gated], _Weight(mix_w_out, (e,)), 0, d, None, F32)
            y_bias = zero_bias
        else:
            o = l // 2
            w_in = _Weight(conv_w_in, (o,))
            z = _dual(h, w_in, w_in, conv_dim, conv_dim, "glu", F32, biases=conv_b_in[o].reshape(1, 2 * conv_dim))
            zc = _conv_mid(z, conv_dw[o], conv_dw_b[o], conv_ln_g[o], conv_ln_b[o], tp, seq, dec_seq)
            y = _mm([zc], _Weight(conv_w_out, (o,)), 0, d, None, F32)
            y_bias = conv_b_out[o]
        x, h = _resid(x, y, y_bias, mod, rows, l, 1, 1.0, alpha, ln_g[l, 1], ln_b[l, 1], (l, 2))

        a = _dual(h, _Weight(ffn_w_gate, (l, 1)), _Weight(ffn_w_up, (l, 1)), 0, d_ff, "swiglu", BF16,
                  pad_cols_to=_kacc_pad(d_ff))
        y = _mm_kacc(a, _Weight(ffn_w_down, (l, 1)))
        if l + 1 < depth:
            x, h = _resid(x, y, zero_bias, mod, rows, l, 2, 0.5, alpha, ln_g[l, 2], ln_b[l, 2], (l + 1, 0))
        else:
            args = (x, y, zero_bias, mod, rows, l, 2, 0.5, alpha, ln_g[l, 2], ln_b[l, 2], None)
            outs = (_resid(*args, row_range=(0, tp)), _resid(*args, row_range=(tp, n_dec * dec_seq)))

    y_prompt = outs[0].reshape(batch, seq, d)
    y_sample = outs[1].reshape(n_dec, dec_seq, d)
    new_cache_k = jnp.stack(new_k, axis=1)
    new_cache_v = jnp.stack(new_v, axis=1)
    return (y_prompt, y_sample, new_cache_k, new_cache_v)
```

```python
import functools
import math
from typing import NamedTuple

import jax
import jax.numpy as jnp
from jax import lax
from jax.experimental import pallas as pl
from jax.experimental.pallas import tpu as pltpu

F32 = jnp.float32
BF16 = jnp.bfloat16

GRID_W = 64
WIN_R = 8
WIN_C = 16
HEAD_DIM = 128
SGU_CHUNK = 128
N_MOD = 9
LN_EPS = 1e-5
NEG_INF = -1e30

V7X_VMEM_BYTES = 64 * 1024 * 1024
V7X_VMEM_LIMIT_CAP = 56 * 1024 * 1024
LANES = 128
SUBLANES = 8
MXU_COLS = 256


def _params(semantics, block_bytes, temp_bytes=0):
    need = 2 * block_bytes + temp_bytes
    limit = min(max(need + need // 4, 32 * 1024 * 1024), V7X_VMEM_LIMIT_CAP)
    return pltpu.CompilerParams(dimension_semantics=semantics, vmem_limit_bytes=int(limit))


def _nbytes(shape, dtype):
    return math.prod(shape) * jnp.dtype(dtype).itemsize


def _tile(dim, pref):
    if dim <= pref:
        return dim
    t = (pref // LANES) * LANES
    while t > LANES and dim % t:
        t -= LANES
    assert dim % t == 0, (dim, pref)
    return t


def _silu(x):
    return x * jax.nn.sigmoid(x)


def _gelu(x):
    return 0.5 * x * (1.0 + lax.erf(x * math.sqrt(0.5)))


def _layer_norm(z, g, b):
    mu = jnp.mean(z, axis=-1, keepdims=True)
    zc = z - mu
    var = jnp.mean(zc * zc, axis=-1, keepdims=True)
    return zc * lax.rsqrt(var + LN_EPS) * g + b


def _adaln_kernel(c_ref, w_ref, b_ref, o_ref):
    s = _silu(c_ref[...]).astype(BF16)
    o_ref[...] = jnp.dot(s, w_ref[...].astype(BF16), preferred_element_type=F32) + b_ref[...]


def _adaln(cond, ada_w, ada_b):
    n_layers, d, n = ada_w.shape
    rows = cond.shape[0]
    tn = _tile(n, 1024)
    blocks = _nbytes((rows, d), F32) + _nbytes((d, tn), F32) + 2 * _nbytes((rows, tn), F32)
    return pl.pallas_call(
        _adaln_kernel,
        grid=(n_layers, n // tn),
        in_specs=[
            pl.BlockSpec((rows, d), lambda l, j: (0, 0)),
            pl.BlockSpec((None, d, tn), lambda l, j: (l, 0, j)),
            pl.BlockSpec((None, 1, tn), lambda l, j: (l, 0, j)),
        ],
        out_specs=pl.BlockSpec((None, rows, tn), lambda l, j: (l, 0, j)),
        out_shape=jax.ShapeDtypeStruct((n_layers, rows, n), F32),
        compiler_params=_params(("arbitrary", "arbitrary"), blocks, _nbytes((d, tn), BF16)),
        name="adaln",
    )(cond, ada_w, ada_b.reshape(n_layers, 1, n))


class _Rows:
    def __init__(self, tp, dec_seq, n_dec):
        self.tp, self.dec_seq, self.n_dec = tp, dec_seq, n_dec
        self.total = tp + dec_seq * n_dec

    def group(self, row):
        return jnp.where(row < self.tp, 0, (row - self.tp) // self.dec_seq + 1)

    def check_block(self, tr):
        assert self.tp % tr == 0 and self.dec_seq % tr == 0, (self.tp, self.dec_seq, tr)


def _mod_spec(rows, layer, tr, d):
    return pl.BlockSpec((None, None, N_MOD, d), lambda i: (layer, rows.group(i * tr), 0, 0))


def _stacked_specs(tr, d, n_first):
    return [pl.BlockSpec((tr, d), lambda i: (jnp.minimum(i, n_first - 1), 0)),
            pl.BlockSpec((tr, d), lambda i: (jnp.maximum(i - n_first, 0), 0))]


def _stacked_pick(n_first, first_ref, second_ref):
    return jnp.where(pl.program_id(0) < n_first, first_ref[...], second_ref[...])


def _modulate_kernel(j, n_first, xa_ref, xb_ref, m_ref, o_ref):
    shift = m_ref[3 * j:3 * j + 1, :]
    scale = m_ref[3 * j + 1:3 * j + 2, :]
    x = _stacked_pick(n_first, xa_ref, xb_ref)
    o_ref[...] = (x * (1.0 + scale) + shift).astype(o_ref.dtype)


def _modulate(xa, xb, mod, rows, layer, j):
    d = xa.shape[1]
    t = xa.shape[0] + xb.shape[0]
    tr = 256
    rows.check_block(tr)
    n_first = xa.shape[0] // tr
    blocks = 2 * _nbytes((tr, d), F32) + _nbytes((tr, d), BF16) + _nbytes((N_MOD, d), F32)
    return pl.pallas_call(
        functools.partial(_modulate_kernel, j, n_first),
        grid=(t // tr,),
        in_specs=_stacked_specs(tr, d, n_first) + [_mod_spec(rows, layer, tr, d)],
        out_specs=pl.BlockSpec((tr, d), lambda i: (i, 0)),
        out_shape=jax.ShapeDtypeStruct((t, d), BF16),
        compiler_params=_params(("arbitrary",), blocks),
        name="modulate",
    )(xa, xb, mod)


_LN_ROWS = 8


def _combine(alpha, weight, gate, x, y, y_bias):
    return alpha * x + (weight * gate) * (y + y_bias)


def _resid_kernel(j, weight, alpha, j_next, x_mode, *refs):
    refs = list(refs)
    if x_mode is None:
        x_refs = []
    elif x_mode == "single":
        x_refs = [refs.pop(0)]
    else:
        x_refs = [refs.pop(0), refs.pop(0)]
    y_ref = refs.pop(0)
    if x_mode is not None:
        yb_ref, m_ref = refs.pop(0), refs.pop(0)
    g_ref, b_ref = refs.pop(0), refs.pop(0)
    if j_next is not None:
        mn_ref = refs.pop(0)
        xo_ref, ho_ref, rows_ref = refs
    else:
        xo_ref, rows_ref = refs

    d = y_ref.shape[1]
    vectors = {"g": g_ref[...], "b": b_ref[...]}
    if x_mode is not None:
        vectors["wg"] = weight * m_ref[3 * j + 2:3 * j + 3, :]
        vectors["yb"] = yb_ref[...]
    if j_next is not None:
        vectors["scale1"] = 1.0 + mn_ref[3 * j_next + 1:3 * j_next + 2, :]
        vectors["shift"] = mn_ref[3 * j_next:3 * j_next + 1, :]
    slot = {name: i for i, name in enumerate(vectors)}
    for name, v in vectors.items():
        rows_ref[slot[name]] = jnp.broadcast_to(v, (_LN_ROWS, d))

    def slab(c, carry):
        r = pl.ds(pl.multiple_of(c * _LN_ROWS, _LN_ROWS), _LN_ROWS)
        if x_mode is None:
            z = y_ref[r, :]
        else:
            if x_mode == "single":
                x = x_refs[0][r, :]
            else:
                x = jnp.where(pl.program_id(0) < x_mode[1], x_refs[0][r, :], x_refs[1][r, :])
            z = alpha * x + rows_ref[slot["wg"]] * (y_ref[r, :] + rows_ref[slot["yb"]])
        xn = _layer_norm(z, rows_ref[slot["g"]], rows_ref[slot["b"]])
        xo_ref[r, :] = xn
        if j_next is not None:
            ho_ref[r, :] = (xn * rows_ref[slot["scale1"]] + rows_ref[slot["shift"]]).astype(ho_ref.dtype)
        return carry

    lax.fori_loop(0, y_ref.shape[0] // _LN_ROWS, slab, 0, unroll=8)


def _resid(x, y, y_bias, mod, rows, layer, j, weight, alpha, ln_g, ln_b, nxt, row_range=None):
    t, d = y.shape
    tr = 256
    rows.check_block(tr)
    start, count = (0, t) if row_range is None else row_range
    assert start % tr == 0 and count % tr == 0
    off = start // tr
    row_spec = pl.BlockSpec((tr, d), lambda i: (i + off, 0))
    vec_spec = pl.BlockSpec((1, d), lambda i: (0, 0))

    def mod_spec(layer_):
        return pl.BlockSpec((None, None, N_MOD, d), lambda i: (layer_, rows.group((i + off) * tr), 0, 0))

    if x is None:
        assert y_bias is None
        x_mode, x_specs, x_args = None, [], []
    elif isinstance(x, tuple):
        assert row_range is None
        n_first = x[0].shape[0] // tr
        x_mode, x_specs, x_args = ("stacked", n_first), _stacked_specs(tr, d, n_first), list(x)
    else:
        x_mode, x_specs, x_args = "single", [row_spec], [x]
    in_specs = x_specs + [row_spec]
    args = x_args + [y]
    if x is not None:
        in_specs += [vec_spec, mod_spec(layer)]
        args += [y_bias.reshape(1, d), mod]
    in_specs += [vec_spec, vec_spec]
    args += [ln_g.reshape(1, d), ln_b.reshape(1, d)]
    out_spec = pl.BlockSpec((tr, d), lambda i: (i, 0))
    n_vectors = 2 + (2 if x is not None else 0) + (2 if nxt is not None else 0)
    blocks = (2 + len(x_args)) * _nbytes((tr, d), F32) + 2 * _nbytes((N_MOD, d), F32)
    if nxt is None:
        out_specs, out_shape, j_next = out_spec, jax.ShapeDtypeStruct((count, d), F32), None
    else:
        in_specs.append(mod_spec(nxt[0]))
        args.append(mod)
        out_specs = [out_spec, out_spec]
        out_shape = [jax.ShapeDtypeStruct((count, d), F32), jax.ShapeDtypeStruct((count, d), BF16)]
        j_next = nxt[1]
        blocks += _nbytes((tr, d), BF16)
    return pl.pallas_call(
        functools.partial(_resid_kernel, j, weight, alpha, j_next, x_mode),
        grid=(count // tr,),
        in_specs=in_specs,
        out_specs=out_specs,
        out_shape=out_shape,
        scratch_shapes=[pltpu.VMEM((n_vectors, _LN_ROWS, d), F32)],
        compiler_params=_params(("arbitrary",), blocks, _nbytes((n_vectors, _LN_ROWS, d), F32)),
        name="resid_ln",
    )(*args)


class _Weight(NamedTuple):
    arr: jax.Array
    lead: tuple = ()

    @property
    def shape(self):
        return self.arr.shape[len(self.lead):]

    def spec(self, block, index_fn):
        lead = self.lead
        return pl.BlockSpec((None,) * len(lead) + tuple(block), lambda *g: lead + tuple(index_fn(*g)))


def _dual_kernel(act, has_bias, n_valid, n_total, h_ref, w1_ref, w2_ref, *rest):
    o_ref = rest[-1]

    def compute():
        h = h_ref[...]
        a = jnp.dot(h, w1_ref[...].astype(BF16), preferred_element_type=F32)
        b = jnp.dot(h, w2_ref[...].astype(BF16), preferred_element_type=F32)
        if has_bias:
            a = a + rest[0][...]
            b = b + rest[1][...]
        out = _silu(a) * b if act == "swiglu" else a * jax.nn.sigmoid(b)
        o_ref[...] = out.astype(o_ref.dtype)

    if n_valid == n_total:
        compute()
    else:
        pl.when(pl.program_id(1) < n_valid)(compute)

        @pl.when(pl.program_id(1) >= n_valid)
        def _():
            o_ref[...] = jnp.zeros_like(o_ref)


def _dual(h, w1, w2, col2_off, n_cols, act, out_dtype, biases=None, pad_cols_to=None):
    t, k = h.shape
    tm = _tile(t, 2048)
    tn = MXU_COLS
    n_out = n_cols if pad_cols_to is None else pad_cols_to
    assert n_cols % tn == 0 and col2_off % tn == 0 and n_out % tn == 0 and n_out >= n_cols
    off2 = col2_off // tn
    n_valid, n_total = n_cols // tn, n_out // tn

    def col(n):
        return jnp.minimum(n, n_valid - 1)

    in_specs = [
        pl.BlockSpec((tm, k), lambda m, n: (m, 0), pipeline_mode=pl.Buffered(1)),
        w1.spec((k, tn), lambda m, n: (0, col(n))),
        w2.spec((k, tn), lambda m, n: (0, col(n) + off2)),
    ]
    args = [h, w1.arr, w2.arr]
    if biases is not None:
        in_specs += [pl.BlockSpec((1, tn), lambda m, n: (0, col(n))),
                     pl.BlockSpec((1, tn), lambda m, n: (0, col(n) + off2))]
        args += [biases, biases]
    blocks = 2 * _nbytes((k, tn), F32) + _nbytes((tm, tn), out_dtype)
    temps = _nbytes((tm, k), BF16) + 2 * _nbytes((k, tn), BF16) + 4 * _nbytes((tm, tn), F32)
    return pl.pallas_call(
        functools.partial(_dual_kernel, act, biases is not None, n_valid, n_total),
        grid=(t // tm, n_total),
        in_specs=in_specs,
        out_specs=pl.BlockSpec((tm, tn), lambda m, n: (m, n)),
        out_shape=jax.ShapeDtypeStruct((t, n_out), out_dtype),
        compiler_params=_params(("arbitrary", "arbitrary"), blocks, temps),
        name="dual_" + act,
    )(*args)


class _Residual(NamedTuple):
    x: jax.Array
    y_bias: jax.Array
    mod: jax.Array
    rows: _Rows
    layer: int
    j: int
    weight: float
    alpha: float


def _mm_kernel(n_pairs, act, residual, *refs):
    o_ref = refs[-1]
    acc = None
    for x_ref, w_ref in zip(refs[:n_pairs], refs[n_pairs:2 * n_pairs]):
        p = jnp.dot(x_ref[...], w_ref[...].astype(BF16), preferred_element_type=F32)
        acc = p if acc is None else acc + p
    if act == "gelu":
        acc = _gelu(acc)
    if residual is not None:
        j, weight, alpha = residual
        x_ref, yb_ref, m_ref = refs[2 * n_pairs:2 * n_pairs + 3]
        acc = _combine(alpha, weight, m_ref[3 * j + 2:3 * j + 3, :], x_ref[...], acc, yb_ref[...])
    o_ref[...] = acc.astype(o_ref.dtype)


def _mm(xs, w, col_off, n_cols, act, out_dtype, residual=None):
    t, k = xs[0].shape
    tm = _tile(t, 1024) if residual is None else _tile(math.gcd(residual.rows.tp, residual.rows.dec_seq), 1024)
    tn = _tile(math.gcd(n_cols, col_off), 512)
    assert all(x.shape == (t, k) for x in xs) and t % tm == 0
    offn = col_off // tn
    in_specs = [pl.BlockSpec((tm, k), lambda m, n: (m, 0)) for _ in xs]
    in_specs += [w.spec((k, tn), functools.partial(lambda i, m, n: (i, n + offn), i)) for i in range(len(xs))]
    args = [*xs, *([w.arr] * len(xs))]
    blocks = len(xs) * (_nbytes((tm, k), BF16) + _nbytes((k, tn), F32)) + _nbytes((tm, tn), out_dtype)
    temps = len(xs) * _nbytes((k, tn), BF16) + 3 * _nbytes((tm, tn), F32)
    consts = None
    if residual is not None:
        r = residual
        assert act is None and col_off == 0 and r.x.shape == (t, n_cols)
        in_specs += [
            pl.BlockSpec((tm, tn), lambda m, n: (m, n)),
            pl.BlockSpec((1, tn), lambda m, n: (0, n)),
            pl.BlockSpec((None, None, N_MOD, tn), lambda m, n: (r.layer, r.rows.group(m * tm), 0, n)),
        ]
        args += [r.x, r.y_bias.reshape(1, n_cols), r.mod]
        blocks += _nbytes((tm, tn), F32) + _nbytes((N_MOD + 1, tn), F32)
        consts = (r.j, r.weight, r.alpha)
    return pl.pallas_call(
        functools.partial(_mm_kernel, len(xs), act, consts),
        grid=(t // tm, n_cols // tn),
        in_specs=in_specs,
        out_specs=pl.BlockSpec((tm, tn), lambda m, n: (m, n)),
        out_shape=jax.ShapeDtypeStruct((t, n_cols), out_dtype),
        compiler_params=_params(("arbitrary", "arbitrary"), blocks, temps),
        name="mm_" + str(act),
    )(*args)


_KACC_SUB = 256
_KACC_NSUB = 4


def _kacc_pad(k):
    step = _KACC_SUB * _KACC_NSUB
    return -(-k // step) * step


def _kacc_kernel(a_ref, *refs):
    w_refs, o_ref = refs[:-1], refs[-1]

    def product():
        w = jnp.concatenate([w_ref[...].astype(BF16) for w_ref in w_refs], axis=0)
        return jnp.dot(a_ref[...], w, preferred_element_type=F32)

    @pl.when(pl.program_id(2) == 0)
    def _():
        o_ref[...] = product()

    @pl.when(pl.program_id(2) != 0)
    def _():
        o_ref[...] += product()


def _mm_kacc(a, w):
    t, k_pad = a.shape
    k, n = w.shape
    assert k % _KACC_SUB == 0 and k_pad == _kacc_pad(k)
    n_sub = k // _KACC_SUB
    tm = _tile(t, 2048)
    tn = _tile(n, 1024)
    tk = _KACC_SUB * _KACC_NSUB
    w_specs = [
        w.spec((_KACC_SUB, tn), functools.partial(
            lambda s, m, j, kk: (jnp.minimum(kk * _KACC_NSUB + s, n_sub - 1), j), s))
        for s in range(_KACC_NSUB)
    ]
    blocks = _nbytes((tm, tk), BF16) + _nbytes((tk, tn), F32) + _nbytes((tm, tn), F32)
    temps = _nbytes((tk, tn), BF16) + _nbytes((tm, tn), F32)
    return pl.pallas_call(
        _kacc_kernel,
        grid=(t // tm, n // tn, k_pad // tk),
        in_specs=[pl.BlockSpec((tm, tk), lambda m, j, kk: (m, kk))] + w_specs,
        out_specs=pl.BlockSpec((tm, tn), lambda m, j, kk: (m, j)),
        out_shape=jax.ShapeDtypeStruct((t, n), F32),
        compiler_params=_params(("arbitrary", "arbitrary", "arbitrary"), blocks, temps),
        name="mm_kacc",
    )(a, *([w.arr] * _KACC_NSUB))


def _heads_kernel(seqs, seq_len, x_ref, w_ref, o_ref):
    acc = jnp.dot(x_ref[...], w_ref[...].astype(BF16), preferred_element_type=F32)
    for s in range(seqs):
        for hh in range(o_ref.shape[1]):
            o_ref[s, hh] = acc[s * seq_len:(s + 1) * seq_len, hh * HEAD_DIM:(hh + 1) * HEAD_DIM]


def _mm_heads(h, w, col_off, n_cols, row_start, n_seq, seq_len):
    k = h.shape[1]
    tm = max(seq_len, _tile(n_seq * seq_len, 1024))
    assert tm % seq_len == 0 and (n_seq * seq_len) % tm == 0 and row_start % tm == 0
    seqs = tm // seq_len
    tn = _tile(math.gcd(n_cols, col_off), 2 * MXU_COLS)
    hpb = tn // HEAD_DIM
    offm, offn = row_start // tm, col_off // tn
    blocks = _nbytes((tm, k), BF16) + _nbytes((k, tn), F32) + _nbytes((tm, tn), F32)
    temps = _nbytes((k, tn), BF16) + 2 * _nbytes((tm, tn), F32)
    return pl.pallas_call(
        functools.partial(_heads_kernel, seqs, seq_len),
        grid=(n_seq // seqs, n_cols // tn),
        in_specs=[pl.BlockSpec((tm, k), lambda m, n: (m + offm, 0)), w.spec((k, tn), lambda m, n: (0, n + offn))],
        out_specs=pl.BlockSpec((seqs, hpb, seq_len, HEAD_DIM), lambda m, n: (m, n, 0, 0)),
        out_shape=jax.ShapeDtypeStruct((n_seq, n_cols // HEAD_DIM, seq_len, HEAD_DIM), F32),
        compiler_params=_params(("arbitrary", "arbitrary"), blocks, temps),
        name="mm_heads",
    )(h, w.arr)


_NT = (((1,), (1,)), ((), ()))


def _ctx_attn_kernel(heads, q_ref, k_ref, v_ref, o_ref):
    scale = HEAD_DIM ** -0.5
    for h in range(heads):
        q = q_ref[h].astype(BF16)
        k = k_ref[h].astype(BF16)
        v = v_ref[h].astype(BF16)
        s = lax.dot_general(q, k, _NT, preferred_element_type=F32) * scale
        p = jnp.exp(s - jnp.max(s, axis=-1, keepdims=True))
        l = jnp.sum(p, axis=-1, keepdims=True)
        o = jnp.dot(p.astype(BF16), v, preferred_element_type=F32) / l
        o_ref[:, h * HEAD_DIM:(h + 1) * HEAD_DIM] = o.astype(o_ref.dtype)


def _ctx_attention(q, k, v):
    b, nh, s, hd = q.shape
    hb = math.gcd(nh, 8)
    spec = pl.BlockSpec((None, hb, s, hd), lambda i, g: (i, g, 0, 0))
    blocks = 3 * _nbytes((hb, s, hd), F32) + _nbytes((s, hb * hd), BF16)
    return pl.pallas_call(
        functools.partial(_ctx_attn_kernel, hb),
        grid=(b, nh // hb),
        in_specs=[spec, spec, spec],
        out_specs=pl.BlockSpec((s, hb * hd), lambda i, g: (i, g)),
        out_shape=jax.ShapeDtypeStruct((b * s, nh * hd), BF16),
        compiler_params=_params(("arbitrary", "arbitrary"), blocks, 8 * _nbytes((s, s), F32)),
        name="ctx_attention",
    )(q, k, v)


def _window_start(i, n, w):
    return min(max(i - w // 2, 0), n - w)


_N_DR = 2 * WIN_R - 1
_N_DC = 2 * WIN_C - 1
_PAIR = LANES // GRID_W


_NA_GROUP = 4
_NA_UNION = 12


class _NaPlan(NamedTuple):
    groups: tuple
    pad_left: int
    n_blocks: int


def _na_plan(rows):
    assert rows % _NA_GROUP == 0 and rows >= _NA_UNION and GRID_W * _PAIR == LANES and _NA_UNION % _PAIR == 0
    groups = []
    for r0 in range(0, rows, _NA_GROUP):
        base = min(_window_start(r0, rows, WIN_R), rows - _NA_UNION)
        bands = []
        for r in range(r0, r0 + _NA_GROUP):
            rs = _window_start(r, rows, WIN_R)
            assert base <= rs and rs + WIN_R <= base + _NA_UNION
            bands.append((rs - base, base - r + WIN_R - 1))
        groups.append((r0, base, tuple(bands)))
    firsts = [first for _, _, bands in groups for _, first in bands]
    pad_left = max(0, -min(firsts))
    pad_right = max(0, max(firsts) + _NA_UNION - _N_DR)
    return _NaPlan(tuple(groups), pad_left, pad_left + _N_DR + pad_right)


def _na_strip_slots(plan):
    return (plan.n_blocks - 1 + _PAIR - 1) // _PAIR


def _na_build_bias(head, plan, rpb_ref, strips_ref, bias_ref):
    lane8 = lax.broadcasted_iota(jnp.int32, (SUBLANES, LANES), 1)
    tap = jnp.clip(lane8 - (GRID_W - 1), -(WIN_C - 1), WIN_C - 1) + (WIN_C - 1)
    table = head * (_N_DR * _N_DC)
    lower, upper = [], []
    for d in range(_N_DR):
        def pick(j, e, d=d):
            return jnp.where(tap == j, rpb_ref[table + d * _N_DC + j], e)

        e = lax.fori_loop(0, _N_DC, pick, jnp.zeros((SUBLANES, LANES), F32))
        e = jnp.broadcast_to(e[0:1, :], (GRID_W, LANES))
        lower.append(pltpu.roll(e, LANES - (GRID_W - 1), 1, stride=1, stride_axis=0))
        upper.append(pltpu.roll(e, 1, 1, stride=1, stride_axis=0))
    zero = jnp.zeros((GRID_W, LANES), F32)
    in_upper = lax.broadcasted_iota(jnp.int32, (GRID_W, LANES), 1) >= GRID_W

    def block(b, halves):
        d = b - plan.pad_left
        return halves[d] if 0 <= d < _N_DR else zero

    for q in range(plan.n_blocks - 1):
        strips_ref[q % _PAIR, :, (q // _PAIR) * LANES:(q // _PAIR + 1) * LANES] = jnp.where(
            in_upper, block(q + 1, upper), block(q, lower))

    shape = (_NA_GROUP * GRID_W, _NA_UNION * GRID_W)
    row = lax.broadcasted_iota(jnp.int32, shape, 0)
    col = lax.broadcasted_iota(jnp.int32, shape, 1)
    band, cq = row // GRID_W, row % GRID_W
    urow, ck = col // GRID_W, col % GRID_W
    cs = jnp.clip(cq - WIN_C // 2, 0, GRID_W - WIN_C)
    col_ok = (ck >= cs) & (ck < cs + WIN_C)
    for g, (_, _, bands) in enumerate(plan.groups):
        lo = jnp.zeros(shape, jnp.int32)
        pieces = []
        for j, (lo_j, first) in enumerate(bands):
            lo = jnp.where(band == j, lo_j, lo)
            s = first + plan.pad_left
            pieces.append(strips_ref[s % _PAIR, :, (s // _PAIR) * LANES:(s // _PAIR) * LANES + shape[1]])
        ok = col_ok & (urow >= lo) & (urow < lo + WIN_R)
        bias_ref[g] = jnp.where(ok, jnp.concatenate(pieces, axis=0), NEG_INF)


def _na_kernel(plan, rpb_ref, q_ref, k_ref, v_ref, ck_ref, cv_ref, o_ref, strips_ref, bias_ref):
    @pl.when(pl.program_id(1) == 0)
    def _():
        _na_build_bias(pl.program_id(0), plan, rpb_ref, strips_ref, bias_ref)

    scale = HEAD_DIM ** -0.5
    kb = k_ref[...].astype(BF16)
    vb = v_ref[...].astype(BF16)
    ckb = ck_ref[...].astype(BF16)
    cvb = cv_ref[...].astype(BF16)
    for g, (r0, base, _) in enumerate(plan.groups):
        queries = slice(r0 * GRID_W, (r0 + _NA_GROUP) * GRID_W)
        keys = slice(base * GRID_W, (base + _NA_UNION) * GRID_W)
        q = q_ref[queries, :].astype(BF16)
        s_loc = lax.dot_general(q, kb[keys], _NT, preferred_element_type=F32) * scale + bias_ref[g]
        s_ctx = lax.dot_general(q, ckb, _NT, preferred_element_type=F32) * scale
        m = jnp.maximum(jnp.max(s_loc, axis=-1, keepdims=True), jnp.max(s_ctx, axis=-1, keepdims=True))
        p_loc = jnp.exp(s_loc - m)
        p_ctx = jnp.exp(s_ctx - m)
        l = jnp.sum(p_loc, axis=-1, keepdims=True) + jnp.sum(p_ctx, axis=-1, keepdims=True)
        o = jnp.dot(p_loc.astype(BF16), vb[keys], preferred_element_type=F32)
        o = o + jnp.dot(p_ctx.astype(BF16), cvb, preferred_element_type=F32)
        o_ref[queries, :] = (o / l).astype(o_ref.dtype)


def _na_attention(qkv, cache_k, cache_v, layer_e, rpb):
    b, nh3, n, hd = qkv.shape
    nh = nh3 // 3
    past = cache_k.shape[3]
    plan = _na_plan(n // GRID_W)
    assert rpb.shape == (nh, _N_DR, _N_DC)
    q_spec, k_spec, v_spec = (
        pl.BlockSpec((None, None, n, hd), functools.partial(lambda part, h, i: (i, part * nh + h, 0, 0), part))
        for part in range(3))
    ctx_spec = pl.BlockSpec((None, None, None, past, hd), lambda h, i: (i, layer_e, h, 0, 0))
    strip_shape = (_PAIR, GRID_W, _na_strip_slots(plan) * LANES)
    bias_shape = (len(plan.groups), _NA_GROUP * GRID_W, _NA_UNION * GRID_W)
    blocks = 3 * _nbytes((n, hd), F32) + 2 * _nbytes((past, hd), F32) + _nbytes((n, hd), BF16)
    scratch = _nbytes(strip_shape, F32) + _nbytes(bias_shape, F32)
    return pl.pallas_call(
        functools.partial(_na_kernel, plan),
        grid=(nh, b),
        in_specs=[pl.BlockSpec(memory_space=pltpu.SMEM), q_spec, k_spec, v_spec, ctx_spec, ctx_spec],
        out_specs=pl.BlockSpec((n, hd), lambda h, i: (i, h)),
        out_shape=jax.ShapeDtypeStruct((b * n, nh * hd), BF16),
        scratch_shapes=[pltpu.VMEM(strip_shape, F32), pltpu.VMEM(bias_shape, F32)],
        compiler_params=_params(("arbitrary", "arbitrary"), blocks, scratch + 8 * _nbytes(bias_shape[1:], F32)),
        name="na_attention",
    )(rpb.reshape(-1), qkv, qkv, qkv, cache_k, cache_v)


def _sgu_kernel(chunks, groups, u_ref, vg_ref, g_ref, b_ref, ws_ref, bs_ref, o_ref):
    gc = u_ref.shape[1] // groups
    for c in range(chunks):
        rows = pl.ds(c * SGU_CHUNK, SGU_CHUNK)
        vn = _layer_norm(vg_ref[rows, :], g_ref[...], b_ref[...]).astype(BF16)
        for g in range(groups):
            cols = slice(g * gc, (g + 1) * gc)
            s = jnp.dot(ws_ref[g].astype(BF16), vn[:, cols], preferred_element_type=F32) + bs_ref[g]
            o_ref[rows, cols] = (u_ref[rows, cols] * s).astype(o_ref.dtype)


def _sgu(ug, ln_g, ln_b, w_s, b_s):
    t, c2 = ug.shape
    c = c2 // 2
    groups = w_s.shape[0]
    tr = 2 * SGU_CHUNK
    assert t % tr == 0 and w_s.shape[1:] == (SGU_CHUNK, SGU_CHUNK)
    vec = pl.BlockSpec((1, c), lambda i: (0, 0))
    blocks = (2 * _nbytes((tr, c), F32) + _nbytes((tr, c), BF16) + _nbytes(w_s.shape, F32)
              + _nbytes((groups, SGU_CHUNK, LANES), F32))
    return pl.pallas_call(
        functools.partial(_sgu_kernel, tr // SGU_CHUNK, groups),
        grid=(t // tr,),
        in_specs=[
            pl.BlockSpec((tr, c), lambda i: (i, 0)),
            pl.BlockSpec((tr, c), lambda i: (i, 1)),
            vec, vec,
            pl.BlockSpec(w_s.shape, lambda i: (0, 0, 0)),
            pl.BlockSpec((groups, SGU_CHUNK, 1), lambda i: (0, 0, 0)),
        ],
        out_specs=pl.BlockSpec((tr, c), lambda i: (i, 0)),
        out_shape=jax.ShapeDtypeStruct((t, c), BF16),
        compiler_params=_params(("arbitrary",), blocks, 4 * _nbytes((SGU_CHUNK, c), F32)),
        name="sgu",
    )(ug, ug, ln_g.reshape(1, c), ln_b.reshape(1, c), w_s, b_s.reshape(groups, SGU_CHUNK, 1))


_CONV_TILE = 256
_CONV_HALO = 16
_CONV_ROWS = 64


def _conv_kernel(tp, tp_seq, dec_seq, taps, prev_ref, cur_ref, next_ref, dw_ref, dwb_ref, g_ref, b_ref, o_ref,
                 zp_ref, y_ref):
    row0 = pl.program_id(0) * _CONV_TILE
    in_ctx = row0 < tp
    seq_len = jnp.where(in_ctx, tp_seq, dec_seq)
    pos = jnp.where(in_ctx, row0, row0 - tp) % seq_len
    has_prev = pos > 0
    has_next = pos + _CONV_TILE < seq_len
    n_blk = cur_ref.shape[1] // LANES
    for cb in range(n_blk):
        cols = slice(cb * LANES, (cb + 1) * LANES)
        zp_ref[cb, 0:_CONV_HALO, :] = jnp.where(has_prev, prev_ref[:, cols], 0.0)
        zp_ref[cb, _CONV_HALO:_CONV_HALO + _CONV_TILE, :] = cur_ref[:, cols]
        zp_ref[cb, _CONV_HALO + _CONV_TILE:, :] = jnp.where(has_next, next_ref[:, cols], 0.0)
    half = taps // 2

    def col_block(cb, carry):
        for rb in range(_CONV_TILE // _CONV_ROWS):
            acc = jnp.broadcast_to(dwb_ref[cb], (_CONV_ROWS, LANES))
            for kk in range(taps):
                start = _CONV_HALO + rb * _CONV_ROWS + kk - half
                acc = acc + zp_ref[cb, pl.ds(start, _CONV_ROWS), :] * dw_ref[cb, kk:kk + 1, :]
            y_ref[cb, rb * _CONV_ROWS:(rb + 1) * _CONV_ROWS, :] = acc
        return carry

    lax.fori_loop(0, n_blk, col_block, 0)

    d = n_blk * LANES
    part = y_ref[0]
    for cb in range(1, n_blk):
        part = part + y_ref[cb]
    mu = jnp.sum(part, axis=-1, keepdims=True) / d
    part = jnp.square(y_ref[0] - mu)
    for cb in range(1, n_blk):
        part = part + jnp.square(y_ref[cb] - mu)
    rstd = lax.rsqrt(jnp.sum(part, axis=-1, keepdims=True) / d + LN_EPS)
    for cb in range(n_blk):
        cols = slice(cb * LANES, (cb + 1) * LANES)
        o_ref[:, cols] = _silu((y_ref[cb] - mu) * rstd * g_ref[:, cols] + b_ref[:, cols]).astype(o_ref.dtype)


def _conv_mid(z, dw, dw_b, ln_g, ln_b, tp, tp_seq, dec_seq):
    t, c = z.shape
    taps = dw.shape[0]
    assert taps // 2 < _CONV_HALO and tp_seq % _CONV_TILE == 0 and dec_seq % _CONV_TILE == 0 and c % LANES == 0
    n_blk = c // LANES
    hb = _CONV_TILE // _CONV_HALO
    last_halo = t // _CONV_HALO - 1
    padded = _CONV_TILE + 2 * _CONV_HALO
    vec = pl.BlockSpec((1, c), lambda i: (0, 0))
    dw_blocks = dw.reshape(taps, n_blk, LANES).transpose(1, 0, 2)
    blocks = (_nbytes((padded, c), F32) + _nbytes((n_blk, 32, LANES), F32) + _nbytes((_CONV_TILE, c), BF16))
    scratch = _nbytes((padded + _CONV_TILE, c), F32)
    return pl.pallas_call(
        functools.partial(_conv_kernel, tp, tp_seq, dec_seq, taps),
        grid=(t // _CONV_TILE,),
        in_specs=[
            pl.BlockSpec((_CONV_HALO, c), lambda i: (jnp.maximum(i * hb - 1, 0), 0)),
            pl.BlockSpec((_CONV_TILE, c), lambda i: (i, 0)),
            pl.BlockSpec((_CONV_HALO, c), lambda i: (jnp.minimum((i + 1) * hb, last_halo), 0)),
            pl.BlockSpec((n_blk, taps, LANES), lambda i: (0, 0, 0)),
            pl.BlockSpec((n_blk, 1, LANES), lambda i: (0, 0, 0)),
            vec, vec,
        ],
        out_specs=pl.BlockSpec((_CONV_TILE, c), lambda i: (i, 0)),
        out_shape=jax.ShapeDtypeStruct((t, c), BF16),
        scratch_shapes=[pltpu.VMEM((n_blk, padded, LANES), F32), pltpu.VMEM((n_blk, _CONV_TILE, LANES), F32)],
        compiler_params=_params(("arbitrary",), blocks, scratch + 2 * _nbytes((_CONV_TILE, c), F32)),
        name="conv_mid",
    )(z, z, z, dw_blocks, dw_b.reshape(n_blk, 1, LANES), ln_g.reshape(1, c), ln_b.reshape(1, c))


def kernel(x_prompt, x_sample, cache_k, cache_v, c, c_ctx, ada_w, ada_b, ln_g, ln_b, ffn_w_gate, ffn_w_up, ffn_w_down, mix_w_in, na_rpb, sgu_ln_g, sgu_ln_b, sgu_w, sgu_b, mix_w_out, conv_w_in, conv_b_in, conv_dw, conv_dw_b, conv_ln_g, conv_ln_b, conv_w_out, conv_b_out):
    batch, seq, d = x_prompt.shape
    n_dec, dec_seq, _ = x_sample.shape
    depth = ada_w.shape[0]
    n_heads = na_rpb.shape[1]
    na_dim = n_heads * HEAD_DIM
    sgu_dim = sgu_ln_g.shape[1]
    conv_dim = conv_dw.shape[2]
    d_ff = ffn_w_gate.shape[3]
    alpha = (2 * depth) ** 0.25
    tp = batch * seq
    rows = _Rows(tp, dec_seq, n_dec)
    n_groups = 1 + n_dec
    assert N_MOD * d == ada_w.shape[2] and mix_w_in.shape[2] == 3 * na_dim + 2 * sgu_dim

    cond = jnp.concatenate([c_ctx[None, :], c, jnp.zeros((SUBLANES - n_groups, d), F32)], axis=0)
    mod = _adaln(cond, ada_w, ada_b)[:, :n_groups].reshape(depth, n_groups, N_MOD, d)

    x = (x_prompt.reshape(tp, d), x_sample.reshape(n_dec * dec_seq, d))
    h = _modulate(x[0], x[1], mod, rows, 0, 0)
    zero_bias = jnp.zeros((d,), F32)
    new_k, new_v = [], []
    outs = None

    for l in range(depth):
        a = _dual(h, _Weight(ffn_w_gate, (l, 0)), _Weight(ffn_w_up, (l, 0)), 0, d_ff, "swiglu", BF16,
                  pad_cols_to=_kacc_pad(d_ff))
        y = _mm_kacc(a, _Weight(ffn_w_down, (l, 0)))
        x, h = _resid(x, y, zero_bias, mod, rows, l, 0, 0.5, alpha, ln_g[l, 0], ln_b[l, 0], (l, 1))

        def residual(y_bias, x=x, l=l):
            return _Residual(x, y_bias, mod, rows, l, 1, 1.0, alpha)

        if l % 2 == 0:
            e = l // 2
            w_in = _Weight(mix_w_in, (e,))
            qp, kp, vp = (_mm_heads(h, w_in, i * na_dim, na_dim, 0, batch, seq) for i in range(3))
            qkv_s = _mm_heads(h, w_in, 0, 3 * na_dim, tp, n_dec, dec_seq)
            new_k.append(kp)
            new_v.append(vp)
            att = jnp.concatenate(
                [_ctx_attention(qp, kp, vp), _na_attention(qkv_s, cache_k, cache_v, e, na_rpb[e])], axis=0)
            ug = _mm([h], w_in, 3 * na_dim, 2 * sgu_dim, "gelu", F32)
            gated = _sgu(ug, sgu_ln_g[e], sgu_ln_b[e], sgu_w[e], sgu_b[e])
            z = _mm([att, gated], _Weight(mix_w_out, (e,)), 0, d, None, F32, residual=residual(zero_bias))
        else:
            o = l // 2
            w_in = _Weight(conv_w_in, (o,))
            glu = _dual(h, w_in, w_in, conv_dim, conv_dim, "glu", F32,
                        biases=conv_b_in[o].reshape(1, 2 * conv_dim))
            zc = _conv_mid(glu, conv_dw[o], conv_dw_b[o], conv_ln_g[o], conv_ln_b[o], tp, seq, dec_seq)
            z = _mm([zc], _Weight(conv_w_out, (o,)), 0, d, None, F32, residual=residual(conv_b_out[o]))
        x, h = _resid(None, z, None, mod, rows, l, 1, 1.0, alpha, ln_g[l, 1], ln_b[l, 1], (l, 2))

        a = _dual(h, _Weight(ffn_w_gate, (l, 1)), _Weight(ffn_w_up, (l, 1)), 0, d_ff, "swiglu", BF16,
                  pad_cols_to=_kacc_pad(d_ff))
        y = _mm_kacc(a, _Weight(ffn_w_down, (l, 1)))
        if l + 1 < depth:
            x, h = _resid(x, y, zero_bias, mod, rows, l, 2, 0.5, alpha, ln_g[l, 2], ln_b[l, 2], (l + 1, 0))
        else:
            args = (x, y, zero_bias, mod, rows, l, 2, 0.5, alpha, ln_g[l, 2], ln_b[l, 2], None)
            outs = (_resid(*args, row_range=(0, tp)), _resid(*args, row_range=(tp, n_dec * dec_seq)))

    y_prompt = outs[0].reshape(batch, seq, d)
    y_sample = outs[1].reshape(n_dec, dec_seq, d)
    new_cache_k = jnp.stack(new_k, axis=1)
    new_cache_v = jnp.stack(new_v, axis=1)
    return (y_prompt, y_sample, new_cache_k, new_cache_v)
```

```python
import functools
import math
from typing import NamedTuple

import jax
import jax.numpy as jnp
from jax import lax
from jax.experimental import pallas as pl
from jax.experimental.pallas import tpu as pltpu

F32 = jnp.float32
BF16 = jnp.bfloat16

GRID_W = 64
WIN_R = 8
WIN_C = 16
HEAD_DIM = 128
SGU_CHUNK = 128
N_MOD = 9
LN_EPS = 1e-5
NEG_INF = -1e30

V7X_VMEM_BYTES = 64 * 1024 * 1024
V7X_VMEM_LIMIT_CAP = 56 * 1024 * 1024
LANES = 128
SUBLANES = 8
MXU_COLS = 256


def _params(semantics, block_bytes, temp_bytes=0):
    need = 2 * block_bytes + temp_bytes
    limit = min(max(need + need // 4, 32 * 1024 * 1024), V7X_VMEM_LIMIT_CAP)
    return pltpu.CompilerParams(dimension_semantics=semantics, vmem_limit_bytes=int(limit))


def _nbytes(shape, dtype):
    return math.prod(shape) * jnp.dtype(dtype).itemsize


def _tile(dim, pref):
    if dim <= pref:
        return dim
    t = (pref // LANES) * LANES
    while t > LANES and dim % t:
        t -= LANES
    assert dim % t == 0, (dim, pref)
    return t


def _silu(x):
    return x * jax.nn.sigmoid(x)


def _gelu(x):
    return 0.5 * x * (1.0 + lax.erf(x * math.sqrt(0.5)))


def _layer_norm(z, g, b):
    mu = jnp.mean(z, axis=-1, keepdims=True)
    zc = z - mu
    var = jnp.mean(zc * zc, axis=-1, keepdims=True)
    return zc * lax.rsqrt(var + LN_EPS) * g + b


def _adaln_kernel(c_ref, w_ref, b_ref, o_ref):
    s = _silu(c_ref[...]).astype(BF16)
    o_ref[...] = jnp.dot(s, w_ref[...].astype(BF16), preferred_element_type=F32) + b_ref[...]


def _adaln(cond, ada_w, ada_b):
    n_layers, d, n = ada_w.shape
    rows = cond.shape[0]
    tn = _tile(n, 1024)
    blocks = _nbytes((rows, d), F32) + _nbytes((d, tn), F32) + 2 * _nbytes((rows, tn), F32)
    return pl.pallas_call(
        _adaln_kernel,
        grid=(n_layers, n // tn),
        in_specs=[
            pl.BlockSpec((rows, d), lambda l, j: (0, 0)),
            pl.BlockSpec((None, d, tn), lambda l, j: (l, 0, j)),
            pl.BlockSpec((None, 1, tn), lambda l, j: (l, 0, j)),
        ],
        out_specs=pl.BlockSpec((None, rows, tn), lambda l, j: (l, 0, j)),
        out_shape=jax.ShapeDtypeStruct((n_layers, rows, n), F32),
        compiler_params=_params(("arbitrary", "arbitrary"), blocks, _nbytes((d, tn), BF16)),
        name="adaln",
    )(cond, ada_w, ada_b.reshape(n_layers, 1, n))


class _Rows:
    def __init__(self, tp, dec_seq, n_dec):
        self.tp, self.dec_seq, self.n_dec = tp, dec_seq, n_dec
        self.total = tp + dec_seq * n_dec

    def group(self, row):
        return jnp.where(row < self.tp, 0, (row - self.tp) // self.dec_seq + 1)

    def check_block(self, tr):
        assert self.tp % tr == 0 and self.dec_seq % tr == 0, (self.tp, self.dec_seq, tr)


def _mod_spec(rows, layer, tr, d):
    return pl.BlockSpec((None, None, N_MOD, d), lambda i: (layer, rows.group(i * tr), 0, 0))


def _stacked_specs(tr, d, n_first):
    return [pl.BlockSpec((tr, d), lambda i: (jnp.minimum(i, n_first - 1), 0)),
            pl.BlockSpec((tr, d), lambda i: (jnp.maximum(i - n_first, 0), 0))]


def _stacked_pick(n_first, first_ref, second_ref):
    return jnp.where(pl.program_id(0) < n_first, first_ref[...], second_ref[...])


def _modulate_kernel(j, n_first, xa_ref, xb_ref, m_ref, o_ref):
    shift = m_ref[3 * j:3 * j + 1, :]
    scale = m_ref[3 * j + 1:3 * j + 2, :]
    x = _stacked_pick(n_first, xa_ref, xb_ref)
    o_ref[...] = (x * (1.0 + scale) + shift).astype(o_ref.dtype)


def _modulate(xa, xb, mod, rows, layer, j):
    d = xa.shape[1]
    t = xa.shape[0] + xb.shape[0]
    tr = 256
    rows.check_block(tr)
    n_first = xa.shape[0] // tr
    blocks = 2 * _nbytes((tr, d), F32) + _nbytes((tr, d), BF16) + _nbytes((N_MOD, d), F32)
    return pl.pallas_call(
        functools.partial(_modulate_kernel, j, n_first),
        grid=(t // tr,),
        in_specs=_stacked_specs(tr, d, n_first) + [_mod_spec(rows, layer, tr, d)],
        out_specs=pl.BlockSpec((tr, d), lambda i: (i, 0)),
        out_shape=jax.ShapeDtypeStruct((t, d), BF16),
        compiler_params=_params(("arbitrary",), blocks),
        name="modulate",
    )(xa, xb, mod)


_LN_ROWS = 8


def _combine(alpha, weight, gate, x, y, y_bias):
    return alpha * x + (weight * gate) * (y + y_bias)


def _resid_kernel(j, weight, alpha, j_next, x_mode, *refs):
    refs = list(refs)
    if x_mode is None:
        x_refs = []
    elif x_mode == "single":
        x_refs = [refs.pop(0)]
    else:
        x_refs = [refs.pop(0), refs.pop(0)]
    y_ref = refs.pop(0)
    if x_mode is not None:
        yb_ref, m_ref = refs.pop(0), refs.pop(0)
    g_ref, b_ref = refs.pop(0), refs.pop(0)
    if j_next is not None:
        mn_ref = refs.pop(0)
        xo_ref, ho_ref, rows_ref = refs
    else:
        xo_ref, rows_ref = refs

    d = y_ref.shape[1]
    vectors = {"g": g_ref[...], "b": b_ref[...]}
    if x_mode is not None:
        vectors["wg"] = weight * m_ref[3 * j + 2:3 * j + 3, :]
        vectors["yb"] = yb_ref[...]
    if j_next is not None:
        vectors["scale1"] = 1.0 + mn_ref[3 * j_next + 1:3 * j_next + 2, :]
        vectors["shift"] = mn_ref[3 * j_next:3 * j_next + 1, :]
    slot = {name: i for i, name in enumerate(vectors)}
    for name, v in vectors.items():
        rows_ref[slot[name]] = jnp.broadcast_to(v, (_LN_ROWS, d))

    def slab(c, carry):
        r = pl.ds(pl.multiple_of(c * _LN_ROWS, _LN_ROWS), _LN_ROWS)
        if x_mode is None:
            z = y_ref[r, :]
        else:
            if x_mode == "single":
                x = x_refs[0][r, :]
            else:
                x = jnp.where(pl.program_id(0) < x_mode[1], x_refs[0][r, :], x_refs[1][r, :])
            z = alpha * x + rows_ref[slot["wg"]] * (y_ref[r, :] + rows_ref[slot["yb"]])
        xn = _layer_norm(z, rows_ref[slot["g"]], rows_ref[slot["b"]])
        xo_ref[r, :] = xn
        if j_next is not None:
            ho_ref[r, :] = (xn * rows_ref[slot["scale1"]] + rows_ref[slot["shift"]]).astype(ho_ref.dtype)
        return carry

    lax.fori_loop(0, y_ref.shape[0] // _LN_ROWS, slab, 0, unroll=8)


def _resid(x, y, y_bias, mod, rows, layer, j, weight, alpha, ln_g, ln_b, nxt, row_range=None):
    t, d = y.shape
    tr = 256
    rows.check_block(tr)
    start, count = (0, t) if row_range is None else row_range
    assert start % tr == 0 and count % tr == 0
    off = start // tr
    row_spec = pl.BlockSpec((tr, d), lambda i: (i + off, 0))
    vec_spec = pl.BlockSpec((1, d), lambda i: (0, 0))

    def mod_spec(layer_):
        return pl.BlockSpec((None, None, N_MOD, d), lambda i: (layer_, rows.group((i + off) * tr), 0, 0))

    if x is None:
        assert y_bias is None
        x_mode, x_specs, x_args = None, [], []
    elif isinstance(x, tuple):
        assert row_range is None
        n_first = x[0].shape[0] // tr
        x_mode, x_specs, x_args = ("stacked", n_first), _stacked_specs(tr, d, n_first), list(x)
    else:
        x_mode, x_specs, x_args = "single", [row_spec], [x]
    in_specs = x_specs + [row_spec]
    args = x_args + [y]
    if x is not None:
        in_specs += [vec_spec, mod_spec(layer)]
        args += [y_bias.reshape(1, d), mod]
    in_specs += [vec_spec, vec_spec]
    args += [ln_g.reshape(1, d), ln_b.reshape(1, d)]
    out_spec = pl.BlockSpec((tr, d), lambda i: (i, 0))
    n_vectors = 2 + (2 if x is not None else 0) + (2 if nxt is not None else 0)
    blocks = (2 + len(x_args)) * _nbytes((tr, d), F32) + 2 * _nbytes((N_MOD, d), F32)
    if nxt is None:
        out_specs, out_shape, j_next = out_spec, jax.ShapeDtypeStruct((count, d), F32), None
    else:
        in_specs.append(mod_spec(nxt[0]))
        args.append(mod)
        out_specs = [out_spec, out_spec]
        out_shape = [jax.ShapeDtypeStruct((count, d), F32), jax.ShapeDtypeStruct((count, d), BF16)]
        j_next = nxt[1]
        blocks += _nbytes((tr, d), BF16)
    return pl.pallas_call(
        functools.partial(_resid_kernel, j, weight, alpha, j_next, x_mode),
        grid=(count // tr,),
        in_specs=in_specs,
        out_specs=out_specs,
        out_shape=out_shape,
        scratch_shapes=[pltpu.VMEM((n_vectors, _LN_ROWS, d), F32)],
        compiler_params=_params(("arbitrary",), blocks, _nbytes((n_vectors, _LN_ROWS, d), F32)),
        name="resid_ln",
    )(*args)


class _Weight(NamedTuple):
    arr: jax.Array
    lead: tuple = ()

    @property
    def shape(self):
        return self.arr.shape[len(self.lead):]

    def spec(self, block, index_fn):
        lead = self.lead
        return pl.BlockSpec((None,) * len(lead) + tuple(block), lambda *g: lead + tuple(index_fn(*g)))


def _dual_kernel(act, has_bias, n_valid, n_total, h_ref, w1_ref, w2_ref, *rest):
    o_ref = rest[-1]

    def compute():
        h = h_ref[...]
        a = jnp.dot(h, w1_ref[...].astype(BF16), preferred_element_type=F32)
        b = jnp.dot(h, w2_ref[...].astype(BF16), preferred_element_type=F32)
        if has_bias:
            a = a + rest[0][...]
            b = b + rest[1][...]
        out = _silu(a) * b if act == "swiglu" else a * jax.nn.sigmoid(b)
        o_ref[...] = out.astype(o_ref.dtype)

    if n_valid == n_total:
        compute()
    else:
        pl.when(pl.program_id(1) < n_valid)(compute)

        @pl.when(pl.program_id(1) >= n_valid)
        def _():
            o_ref[...] = jnp.zeros_like(o_ref)


def _dual(h, w1, w2, col2_off, n_cols, act, out_dtype, biases=None, pad_cols_to=None):
    t, k = h.shape
    tm = _tile(t, 2048)
    tn = MXU_COLS
    n_out = n_cols if pad_cols_to is None else pad_cols_to
    assert n_cols % tn == 0 and col2_off % tn == 0 and n_out % tn == 0 and n_out >= n_cols
    off2 = col2_off // tn
    n_valid, n_total = n_cols // tn, n_out // tn

    def col(n):
        return jnp.minimum(n, n_valid - 1)

    in_specs = [
        pl.BlockSpec((tm, k), lambda m, n: (m, 0), pipeline_mode=pl.Buffered(1)),
        w1.spec((k, tn), lambda m, n: (0, col(n))),
        w2.spec((k, tn), lambda m, n: (0, col(n) + off2)),
    ]
    args = [h, w1.arr, w2.arr]
    if biases is not None:
        in_specs += [pl.BlockSpec((1, tn), lambda m, n: (0, col(n))),
                     pl.BlockSpec((1, tn), lambda m, n: (0, col(n) + off2))]
        args += [biases, biases]
    blocks = 2 * _nbytes((k, tn), F32) + _nbytes((tm, tn), out_dtype)
    temps = _nbytes((tm, k), BF16) + 2 * _nbytes((k, tn), BF16) + 4 * _nbytes((tm, tn), F32)
    return pl.pallas_call(
        functools.partial(_dual_kernel, act, biases is not None, n_valid, n_total),
        grid=(t // tm, n_total),
        in_specs=in_specs,
        out_specs=pl.BlockSpec((tm, tn), lambda m, n: (m, n)),
        out_shape=jax.ShapeDtypeStruct((t, n_out), out_dtype),
        compiler_params=_params(("arbitrary", "arbitrary"), blocks, temps),
        name="dual_" + act,
    )(*args)


class _Residual(NamedTuple):
    x: jax.Array
    y_bias: jax.Array
    mod: jax.Array
    rows: _Rows
    layer: int
    j: int
    weight: float
    alpha: float


def _mm_kernel(n_pairs, act, residual, *refs):
    o_ref = refs[-1]
    acc = None
    for x_ref, w_ref in zip(refs[:n_pairs], refs[n_pairs:2 * n_pairs]):
        p = jnp.dot(x_ref[...], w_ref[...].astype(BF16), preferred_element_type=F32)
        acc = p if acc is None else acc + p
    if act == "gelu":
        acc = _gelu(acc)
    if residual is not None:
        j, weight, alpha = residual
        x_ref, yb_ref, m_ref = refs[2 * n_pairs:2 * n_pairs + 3]
        acc = _combine(alpha, weight, m_ref[3 * j + 2:3 * j + 3, :], x_ref[...], acc, yb_ref[...])
    o_ref[...] = acc.astype(o_ref.dtype)


def _mm(xs, w, col_off, n_cols, act, out_dtype, residual=None):
    t, k = xs[0].shape
    tm = _tile(t, 1024) if residual is None else _tile(math.gcd(residual.rows.tp, residual.rows.dec_seq), 1024)
    tn = _tile(math.gcd(n_cols, col_off), 512)
    assert all(x.shape == (t, k) for x in xs) and t % tm == 0
    offn = col_off // tn
    in_specs = [pl.BlockSpec((tm, k), lambda m, n: (m, 0)) for _ in xs]
    in_specs += [w.spec((k, tn), functools.partial(lambda i, m, n: (i, n + offn), i)) for i in range(len(xs))]
    args = [*xs, *([w.arr] * len(xs))]
    blocks = len(xs) * (_nbytes((tm, k), BF16) + _nbytes((k, tn), F32)) + _nbytes((tm, tn), out_dtype)
    temps = len(xs) * _nbytes((k, tn), BF16) + 3 * _nbytes((tm, tn), F32)
    consts = None
    if residual is not None:
        r = residual
        assert act is None and col_off == 0 and r.x.shape == (t, n_cols)
        in_specs += [
            pl.BlockSpec((tm, tn), lambda m, n: (m, n)),
            pl.BlockSpec((1, tn), lambda m, n: (0, n)),
            pl.BlockSpec((None, None, N_MOD, tn), lambda m, n: (r.layer, r.rows.group(m * tm), 0, n)),
        ]
        args += [r.x, r.y_bias.reshape(1, n_cols), r.mod]
        blocks += _nbytes((tm, tn), F32) + _nbytes((N_MOD + 1, tn), F32)
        consts = (r.j, r.weight, r.alpha)
    return pl.pallas_call(
        functools.partial(_mm_kernel, len(xs), act, consts),
        grid=(t // tm, n_cols // tn),
        in_specs=in_specs,
        out_specs=pl.BlockSpec((tm, tn), lambda m, n: (m, n)),
        out_shape=jax.ShapeDtypeStruct((t, n_cols), out_dtype),
        compiler_params=_params(("arbitrary", "arbitrary"), blocks, temps),
        name="mm_" + str(act),
    )(*args)


_KACC_SUB = 256
_KACC_NSUB = 4


def _kacc_pad(k):
    step = _KACC_SUB * _KACC_NSUB
    return -(-k // step) * step


def _kacc_kernel(residual, a_ref, *refs):
    w_refs, refs = refs[:_KACC_NSUB], refs[_KACC_NSUB:]
    kk, last = pl.program_id(2), pl.num_programs(2) - 1

    def product():
        w = jnp.concatenate([w_ref[...].astype(BF16) for w_ref in w_refs], axis=0)
        return jnp.dot(a_ref[...], w, preferred_element_type=F32)

    if residual is None:
        (o_ref,) = refs

        @pl.when(kk == 0)
        def _():
            o_ref[...] = product()

        @pl.when(kk != 0)
        def _():
            o_ref[...] += product()
        return

    j, weight, alpha, part_rows = residual
    x_hbm, *m_refs, o_ref, x_buf, sem = refs
    tm, tn = o_ref.shape

    def x_copy():
        rows = pl.ds(pl.multiple_of(pl.program_id(0) * tm, tm), tm)
        cols = pl.ds(pl.multiple_of(pl.program_id(1) * tn, tn), tn)
        return pltpu.make_async_copy(x_hbm.at[rows, cols], x_buf, sem)

    @pl.when(kk == 0)
    def _():
        x_copy().start()
        o_ref[...] = product()

    @pl.when((kk != 0) & (kk != last))
    def _():
        o_ref[...] += product()

    @pl.when(kk == last)
    def _():
        y = o_ref[...] + product()
        x_copy().wait()
        for p, m_ref in enumerate(m_refs):
            rows = slice(p * part_rows, (p + 1) * part_rows)
            gate = m_ref[3 * j + 2:3 * j + 3, :]
            o_ref[rows, :] = alpha * x_buf[rows, :] + (weight * gate) * y[rows, :]


def _mm_kacc(a, w, residual=None):
    t, k_pad = a.shape
    k, n = w.shape
    assert k % _KACC_SUB == 0 and k_pad == _kacc_pad(k)
    n_sub = k // _KACC_SUB
    tn = _tile(n, 1024)
    tk = _KACC_SUB * _KACC_NSUB
    if residual is None:
        tm = _tile(t, 2048)
    else:
        assert k_pad // tk >= 2
        r = residual
        part = _tile(math.gcd(r.rows.tp, r.rows.dec_seq), 1024)
        tm = 2 * part if t % (2 * part) == 0 else part
        assert r.y_bias is None and r.x.shape == (t, n) and t % tm == 0
    w_specs = [
        w.spec((_KACC_SUB, tn), functools.partial(
            lambda s, m, j, kk: (jnp.minimum(kk * _KACC_NSUB + s, n_sub - 1), j), s))
        for s in range(_KACC_NSUB)
    ]
    in_specs = [pl.BlockSpec((tm, tk), lambda m, j, kk: (m, kk))] + w_specs
    args = [a, *([w.arr] * _KACC_NSUB)]
    blocks = _nbytes((tm, tk), BF16) + _nbytes((tk, tn), F32) + _nbytes((tm, tn), F32)
    temps = _nbytes((tk, tn), BF16) + _nbytes((tm, tn), F32)
    consts, scratch = None, []
    if residual is not None:
        n_parts = tm // part
        in_specs.append(pl.BlockSpec(memory_space=pl.ANY))
        in_specs += [
            pl.BlockSpec((None, None, N_MOD, tn), functools.partial(
                lambda p, m, j, kk: (r.layer, r.rows.group(m * tm + p * part), 0, j), p))
            for p in range(n_parts)
        ]
        args += [r.x, *([r.mod] * n_parts)]
        scratch = [pltpu.VMEM((tm, tn), F32), pltpu.SemaphoreType.DMA(())]
        temps += _nbytes((tm, tn), F32)
        consts = (r.j, r.weight, r.alpha, part)
    return pl.pallas_call(
        functools.partial(_kacc_kernel, consts),
        grid=(t // tm, n // tn, k_pad // tk),
        in_specs=in_specs,
        out_specs=pl.BlockSpec((tm, tn), lambda m, j, kk: (m, j)),
        out_shape=jax.ShapeDtypeStruct((t, n), F32),
        scratch_shapes=scratch,
        compiler_params=_params(("arbitrary", "arbitrary", "arbitrary"), blocks, temps),
        name="mm_kacc",
    )(*args)


def _heads_kernel(seqs, seq_len, x_ref, w_ref, o_ref):
    acc = jnp.dot(x_ref[...], w_ref[...].astype(BF16), preferred_element_type=F32)
    for s in range(seqs):
        for hh in range(o_ref.shape[1]):
            o_ref[s, hh] = acc[s * seq_len:(s + 1) * seq_len, hh * HEAD_DIM:(hh + 1) * HEAD_DIM]


def _mm_heads(h, w, col_off, n_cols, row_start, n_seq, seq_len):
    k = h.shape[1]
    tm = max(seq_len, _tile(n_seq * seq_len, 1024))
    assert tm % seq_len == 0 and (n_seq * seq_len) % tm == 0 and row_start % tm == 0
    seqs = tm // seq_len
    tn = _tile(math.gcd(n_cols, col_off), 2 * MXU_COLS)
    hpb = tn // HEAD_DIM
    offm, offn = row_start // tm, col_off // tn
    blocks = _nbytes((tm, k), BF16) + _nbytes((k, tn), F32) + _nbytes((tm, tn), F32)
    temps = _nbytes((k, tn), BF16) + 2 * _nbytes((tm, tn), F32)
    return pl.pallas_call(
        functools.partial(_heads_kernel, seqs, seq_len),
        grid=(n_seq // seqs, n_cols // tn),
        in_specs=[pl.BlockSpec((tm, k), lambda m, n: (m + offm, 0)), w.spec((k, tn), lambda m, n: (0, n + offn))],
        out_specs=pl.BlockSpec((seqs, hpb, seq_len, HEAD_DIM), lambda m, n: (m, n, 0, 0)),
        out_shape=jax.ShapeDtypeStruct((n_seq, n_cols // HEAD_DIM, seq_len, HEAD_DIM), F32),
        compiler_params=_params(("arbitrary", "arbitrary"), blocks, temps),
        name="mm_heads",
    )(h, w.arr)


_NT = (((1,), (1,)), ((), ()))


def _ctx_attn_kernel(heads, q_ref, k_ref, v_ref, o_ref):
    scale = HEAD_DIM ** -0.5
    for h in range(heads):
        q = q_ref[h].astype(BF16)
        k = k_ref[h].astype(BF16)
        v = v_ref[h].astype(BF16)
        s = lax.dot_general(q, k, _NT, preferred_element_type=F32) * scale
        p = jnp.exp(s - jnp.max(s, axis=-1, keepdims=True))
        l = jnp.sum(p, axis=-1, keepdims=True)
        o = jnp.dot(p.astype(BF16), v, preferred_element_type=F32) / l
        o_ref[:, h * HEAD_DIM:(h + 1) * HEAD_DIM] = o.astype(o_ref.dtype)


def _ctx_attention(q, k, v):
    b, nh, s, hd = q.shape
    hb = math.gcd(nh, 8)
    spec = pl.BlockSpec((None, hb, s, hd), lambda i, g: (i, g, 0, 0))
    blocks = 3 * _nbytes((hb, s, hd), F32) + _nbytes((s, hb * hd), BF16)
    return pl.pallas_call(
        functools.partial(_ctx_attn_kernel, hb),
        grid=(b, nh // hb),
        in_specs=[spec, spec, spec],
        out_specs=pl.BlockSpec((s, hb * hd), lambda i, g: (i, g)),
        out_shape=jax.ShapeDtypeStruct((b * s, nh * hd), BF16),
        compiler_params=_params(("arbitrary", "arbitrary"), blocks, 8 * _nbytes((s, s), F32)),
        name="ctx_attention",
    )(q, k, v)


def _window_start(i, n, w):
    return min(max(i - w // 2, 0), n - w)


_N_DR = 2 * WIN_R - 1
_N_DC = 2 * WIN_C - 1
_PAIR = LANES // GRID_W


_NA_GROUP = 4
_NA_UNION = 12


class _NaPlan(NamedTuple):
    groups: tuple
    pad_left: int
    n_blocks: int


def _na_plan(rows):
    assert rows % _NA_GROUP == 0 and rows >= _NA_UNION and GRID_W * _PAIR == LANES and _NA_UNION % _PAIR == 0
    groups = []
    for r0 in range(0, rows, _NA_GROUP):
        base = min(_window_start(r0, rows, WIN_R), rows - _NA_UNION)
        bands = []
        for r in range(r0, r0 + _NA_GROUP):
            rs = _window_start(r, rows, WIN_R)
            assert base <= rs and rs + WIN_R <= base + _NA_UNION
            bands.append((rs - base, base - r + WIN_R - 1))
        groups.append((r0, base, tuple(bands)))
    firsts = [first for _, _, bands in groups for _, first in bands]
    pad_left = max(0, -min(firsts))
    pad_right = max(0, max(firsts) + _NA_UNION - _N_DR)
    return _NaPlan(tuple(groups), pad_left, pad_left + _N_DR + pad_right)


def _na_strip_slots(plan):
    return (plan.n_blocks - 1 + _PAIR - 1) // _PAIR


def _na_build_bias(head, plan, rpb_ref, strips_ref, bias_ref):
    lane8 = lax.broadcasted_iota(jnp.int32, (SUBLANES, LANES), 1)
    tap = jnp.clip(lane8 - (GRID_W - 1), -(WIN_C - 1), WIN_C - 1) + (WIN_C - 1)
    table = head * (_N_DR * _N_DC)
    lower, upper = [], []
    for d in range(_N_DR):
        def pick(j, e, d=d):
            return jnp.where(tap == j, rpb_ref[table + d * _N_DC + j], e)

        e = lax.fori_loop(0, _N_DC, pick, jnp.zeros((SUBLANES, LANES), F32))
        e = jnp.broadcast_to(e[0:1, :], (GRID_W, LANES))
        lower.append(pltpu.roll(e, LANES - (GRID_W - 1), 1, stride=1, stride_axis=0))
        upper.append(pltpu.roll(e, 1, 1, stride=1, stride_axis=0))
    zero = jnp.zeros((GRID_W, LANES), F32)
    in_upper = lax.broadcasted_iota(jnp.int32, (GRID_W, LANES), 1) >= GRID_W

    def block(b, halves):
        d = b - plan.pad_left
        return halves[d] if 0 <= d < _N_DR else zero

    for q in range(plan.n_blocks - 1):
        strips_ref[q % _PAIR, :, (q // _PAIR) * LANES:(q // _PAIR + 1) * LANES] = jnp.where(
            in_upper, block(q + 1, upper), block(q, lower))

    shape = (_NA_GROUP * GRID_W, _NA_UNION * GRID_W)
    row = lax.broadcasted_iota(jnp.int32, shape, 0)
    col = lax.broadcasted_iota(jnp.int32, shape, 1)
    band, cq = row // GRID_W, row % GRID_W
    urow, ck = col // GRID_W, col % GRID_W
    cs = jnp.clip(cq - WIN_C // 2, 0, GRID_W - WIN_C)
    col_ok = (ck >= cs) & (ck < cs + WIN_C)
    for g, (_, _, bands) in enumerate(plan.groups):
        lo = jnp.zeros(shape, jnp.int32)
        pieces = []
        for j, (lo_j, first) in enumerate(bands):
            lo = jnp.where(band == j, lo_j, lo)
            s = first + plan.pad_left
            pieces.append(strips_ref[s % _PAIR, :, (s // _PAIR) * LANES:(s // _PAIR) * LANES + shape[1]])
        ok = col_ok & (urow >= lo) & (urow < lo + WIN_R)
        bias_ref[g] = jnp.where(ok, jnp.concatenate(pieces, axis=0), NEG_INF)


def _na_kernel(plan, rpb_ref, q_ref, k_ref, v_ref, ck_ref, cv_ref, o_ref, strips_ref, bias_ref):
    @pl.when(pl.program_id(1) == 0)
    def _():
        _na_build_bias(pl.program_id(0), plan, rpb_ref, strips_ref, bias_ref)

    scale = HEAD_DIM ** -0.5
    kb = k_ref[...].astype(BF16)
    vb = v_ref[...].astype(BF16)
    ckb = ck_ref[...].astype(BF16)
    cvb = cv_ref[...].astype(BF16)
    for g, (r0, base, _) in enumerate(plan.groups):
        queries = slice(r0 * GRID_W, (r0 + _NA_GROUP) * GRID_W)
        keys = slice(base * GRID_W, (base + _NA_UNION) * GRID_W)
        q = q_ref[queries, :].astype(BF16)
        s_loc = lax.dot_general(q, kb[keys], _NT, preferred_element_type=F32) * scale + bias_ref[g]
        s_ctx = lax.dot_general(q, ckb, _NT, preferred_element_type=F32) * scale
        m = jnp.maximum(jnp.max(s_loc, axis=-1, keepdims=True), jnp.max(s_ctx, axis=-1, keepdims=True))
        p_loc = jnp.exp(s_loc - m)
        p_ctx = jnp.exp(s_ctx - m)
        l = jnp.sum(p_loc, axis=-1, keepdims=True) + jnp.sum(p_ctx, axis=-1, keepdims=True)
        o = jnp.dot(p_loc.astype(BF16), vb[keys], preferred_element_type=F32)
        o = o + jnp.dot(p_ctx.astype(BF16), cvb, preferred_element_type=F32)
        o_ref[queries, :] = (o / l).astype(o_ref.dtype)


def _na_attention(qkv, cache_k, cache_v, layer_e, rpb):
    b, nh3, n, hd = qkv.shape
    nh = nh3 // 3
    past = cache_k.shape[3]
    plan = _na_plan(n // GRID_W)
    assert rpb.shape == (nh, _N_DR, _N_DC)
    q_spec, k_spec, v_spec = (
        pl.BlockSpec((None, None, n, hd), functools.partial(lambda part, h, i: (i, part * nh + h, 0, 0), part))
        for part in range(3))
    ctx_spec = pl.BlockSpec((None, None, None, past, hd), lambda h, i: (i, layer_e, h, 0, 0))
    strip_shape = (_PAIR, GRID_W, _na_strip_slots(plan) * LANES)
    bias_shape = (len(plan.groups), _NA_GROUP * GRID_W, _NA_UNION * GRID_W)
    blocks = 3 * _nbytes((n, hd), F32) + 2 * _nbytes((past, hd), F32) + _nbytes((n, hd), BF16)
    scratch = _nbytes(strip_shape, F32) + _nbytes(bias_shape, F32)
    return pl.pallas_call(
        functools.partial(_na_kernel, plan),
        grid=(nh, b),
        in_specs=[pl.BlockSpec(memory_space=pltpu.SMEM), q_spec, k_spec, v_spec, ctx_spec, ctx_spec],
        out_specs=pl.BlockSpec((n, hd), lambda h, i: (i, h)),
        out_shape=jax.ShapeDtypeStruct((b * n, nh * hd), BF16),
        scratch_shapes=[pltpu.VMEM(strip_shape, F32), pltpu.VMEM(bias_shape, F32)],
        compiler_params=_params(("arbitrary", "arbitrary"), blocks, scratch + 8 * _nbytes(bias_shape[1:], F32)),
        name="na_attention",
    )(rpb.reshape(-1), qkv, qkv, qkv, cache_k, cache_v)


def _sgu_kernel(chunks, groups, u_ref, vg_ref, g_ref, b_ref, ws_ref, bs_ref, o_ref):
    gc = u_ref.shape[1] // groups
    for c in range(chunks):
        rows = pl.ds(c * SGU_CHUNK, SGU_CHUNK)
        vn = _layer_norm(vg_ref[rows, :], g_ref[...], b_ref[...]).astype(BF16)
        for g in range(groups):
            cols = slice(g * gc, (g + 1) * gc)
            s = jnp.dot(ws_ref[g].astype(BF16), vn[:, cols], preferred_element_type=F32) + bs_ref[g]
            o_ref[rows, cols] = (u_ref[rows, cols] * s).astype(o_ref.dtype)


def _sgu(ug, ln_g, ln_b, w_s, b_s):
    t, c2 = ug.shape
    c = c2 // 2
    groups = w_s.shape[0]
    tr = 2 * SGU_CHUNK
    assert t % tr == 0 and w_s.shape[1:] == (SGU_CHUNK, SGU_CHUNK)
    vec = pl.BlockSpec((1, c), lambda i: (0, 0))
    blocks = (2 * _nbytes((tr, c), F32) + _nbytes((tr, c), BF16) + _nbytes(w_s.shape, F32)
              + _nbytes((groups, SGU_CHUNK, LANES), F32))
    return pl.pallas_call(
        functools.partial(_sgu_kernel, tr // SGU_CHUNK, groups),
        grid=(t // tr,),
        in_specs=[
            pl.BlockSpec((tr, c), lambda i: (i, 0)),
            pl.BlockSpec((tr, c), lambda i: (i, 1)),
            vec, vec,
            pl.BlockSpec(w_s.shape, lambda i: (0, 0, 0)),
            pl.BlockSpec((groups, SGU_CHUNK, 1), lambda i: (0, 0, 0)),
        ],
        out_specs=pl.BlockSpec((tr, c), lambda i: (i, 0)),
        out_shape=jax.ShapeDtypeStruct((t, c), BF16),
        compiler_params=_params(("arbitrary",), blocks, 4 * _nbytes((SGU_CHUNK, c), F32)),
        name="sgu",
    )(ug, ug, ln_g.reshape(1, c), ln_b.reshape(1, c), w_s, b_s.reshape(groups, SGU_CHUNK, 1))


_CONV_TILE = 256
_CONV_HALO = 16
_CONV_ROWS = 64


def _conv_kernel(tp, tp_seq, dec_seq, taps, prev_ref, cur_ref, next_ref, dw_ref, dwb_ref, g_ref, b_ref, o_ref,
                 zp_ref, y_ref):
    row0 = pl.program_id(0) * _CONV_TILE
    in_ctx = row0 < tp
    seq_len = jnp.where(in_ctx, tp_seq, dec_seq)
    pos = jnp.where(in_ctx, row0, row0 - tp) % seq_len
    has_prev = pos > 0
    has_next = pos + _CONV_TILE < seq_len
    n_blk = cur_ref.shape[1] // LANES
    for cb in range(n_blk):
        cols = slice(cb * LANES, (cb + 1) * LANES)
        zp_ref[cb, 0:_CONV_HALO, :] = jnp.where(has_prev, prev_ref[:, cols], 0.0)
        zp_ref[cb, _CONV_HALO:_CONV_HALO + _CONV_TILE, :] = cur_ref[:, cols]
        zp_ref[cb, _CONV_HALO + _CONV_TILE:, :] = jnp.where(has_next, next_ref[:, cols], 0.0)
    half = taps // 2

    def col_block(cb, carry):
        for rb in range(_CONV_TILE // _CONV_ROWS):
            acc = jnp.broadcast_to(dwb_ref[cb], (_CONV_ROWS, LANES))
            for kk in range(taps):
                start = _CONV_HALO + rb * _CONV_ROWS + kk - half
                acc = acc + zp_ref[cb, pl.ds(start, _CONV_ROWS), :] * dw_ref[cb, kk:kk + 1, :]
            y_ref[cb, rb * _CONV_ROWS:(rb + 1) * _CONV_ROWS, :] = acc
        return carry

    lax.fori_loop(0, n_blk, col_block, 0)

    d = n_blk * LANES
    part = y_ref[0]
    for cb in range(1, n_blk):
        part = part + y_ref[cb]
    mu = jnp.sum(part, axis=-1, keepdims=True) / d
    part = jnp.square(y_ref[0] - mu)
    for cb in range(1, n_blk):
        part = part + jnp.square(y_ref[cb] - mu)
    rstd = lax.rsqrt(jnp.sum(part, axis=-1, keepdims=True) / d + LN_EPS)
    for cb in range(n_blk):
        cols = slice(cb * LANES, (cb + 1) * LANES)
        o_ref[:, cols] = _silu((y_ref[cb] - mu) * rstd * g_ref[:, cols] + b_ref[:, cols]).astype(o_ref.dtype)


def _conv_mid(z, dw, dw_b, ln_g, ln_b, tp, tp_seq, dec_seq):
    t, c = z.shape
    taps = dw.shape[0]
    assert taps // 2 < _CONV_HALO and tp_seq % _CONV_TILE == 0 and dec_seq % _CONV_TILE == 0 and c % LANES == 0
    n_blk = c // LANES
    hb = _CONV_TILE // _CONV_HALO
    last_halo = t // _CONV_HALO - 1
    padded = _CONV_TILE + 2 * _CONV_HALO
    vec = pl.BlockSpec((1, c), lambda i: (0, 0))
    dw_blocks = dw.reshape(taps, n_blk, LANES).transpose(1, 0, 2)
    blocks = (_nbytes((padded, c), F32) + _nbytes((n_blk, 32, LANES), F32) + _nbytes((_CONV_TILE, c), BF16))
    scratch = _nbytes((padded + _CONV_TILE, c), F32)
    return pl.pallas_call(
        functools.partial(_conv_kernel, tp, tp_seq, dec_seq, taps),
        grid=(t // _CONV_TILE,),
        in_specs=[
            pl.BlockSpec((_CONV_HALO, c), lambda i: (jnp.maximum(i * hb - 1, 0), 0)),
            pl.BlockSpec((_CONV_TILE, c), lambda i: (i, 0)),
            pl.BlockSpec((_CONV_HALO, c), lambda i: (jnp.minimum((i + 1) * hb, last_halo), 0)),
            pl.BlockSpec((n_blk, taps, LANES), lambda i: (0, 0, 0)),
            pl.BlockSpec((n_blk, 1, LANES), lambda i: (0, 0, 0)),
            vec, vec,
        ],
        out_specs=pl.BlockSpec((_CONV_TILE, c), lambda i: (i, 0)),
        out_shape=jax.ShapeDtypeStruct((t, c), BF16),
        scratch_shapes=[pltpu.VMEM((n_blk, padded, LANES), F32), pltpu.VMEM((n_blk, _CONV_TILE, LANES), F32)],
        compiler_params=_params(("arbitrary",), blocks, scratch + 2 * _nbytes((_CONV_TILE, c), F32)),
        name="conv_mid",
    )(z, z, z, dw_blocks, dw_b.reshape(n_blk, 1, LANES), ln_g.reshape(1, c), ln_b.reshape(1, c))


def kernel(x_prompt, x_sample, cache_k, cache_v, c, c_ctx, ada_w, ada_b, ln_g, ln_b, ffn_w_gate, ffn_w_up, ffn_w_down, mix_w_in, na_rpb, sgu_ln_g, sgu_ln_b, sgu_w, sgu_b, mix_w_out, conv_w_in, conv_b_in, conv_dw, conv_dw_b, conv_ln_g, conv_ln_b, conv_w_out, conv_b_out):
    batch, seq, d = x_prompt.shape
    n_dec, dec_seq, _ = x_sample.shape
    depth = ada_w.shape[0]
    n_heads = na_rpb.shape[1]
    na_dim = n_heads * HEAD_DIM
    sgu_dim = sgu_ln_g.shape[1]
    conv_dim = conv_dw.shape[2]
    d_ff = ffn_w_gate.shape[3]
    alpha = (2 * depth) ** 0.25
    tp = batch * seq
    rows = _Rows(tp, dec_seq, n_dec)
    n_groups = 1 + n_dec
    assert N_MOD * d == ada_w.shape[2] and mix_w_in.shape[2] == 3 * na_dim + 2 * sgu_dim

    cond = jnp.concatenate([c_ctx[None, :], c, jnp.zeros((SUBLANES - n_groups, d), F32)], axis=0)
    mod = _adaln(cond, ada_w, ada_b)[:, :n_groups].reshape(depth, n_groups, N_MOD, d)

    x = (x_prompt.reshape(tp, d), x_sample.reshape(n_dec * dec_seq, d))
    h = _modulate(x[0], x[1], mod, rows, 0, 0)
    zero_bias = jnp.zeros((d,), F32)
    new_k, new_v = [], []
    outs = None

    for l in range(depth):
        a = _dual(h, _Weight(ffn_w_gate, (l, 0)), _Weight(ffn_w_up, (l, 0)), 0, d_ff, "swiglu", BF16,
                  pad_cols_to=_kacc_pad(d_ff))
        if isinstance(x, tuple):
            y = _mm_kacc(a, _Weight(ffn_w_down, (l, 0)))
            x, h = _resid(x, y, zero_bias, mod, rows, l, 0, 0.5, alpha, ln_g[l, 0], ln_b[l, 0], (l, 1))
        else:
            z = _mm_kacc(a, _Weight(ffn_w_down, (l, 0)), residual=_Residual(x, None, mod, rows, l, 0, 0.5, alpha))
            x, h = _resid(None, z, None, mod, rows, l, 0, 0.5, alpha, ln_g[l, 0], ln_b[l, 0], (l, 1))

        def residual(y_bias, x=x, l=l):
            return _Residual(x, y_bias, mod, rows, l, 1, 1.0, alpha)

        if l % 2 == 0:
            e = l // 2
            w_in = _Weight(mix_w_in, (e,))
            qp, kp, vp = (_mm_heads(h, w_in, i * na_dim, na_dim, 0, batch, seq) for i in range(3))
            qkv_s = _mm_heads(h, w_in, 0, 3 * na_dim, tp, n_dec, dec_seq)
            new_k.append(kp)
            new_v.append(vp)
            att = jnp.concatenate(
                [_ctx_attention(qp, kp, vp), _na_attention(qkv_s, cache_k, cache_v, e, na_rpb[e])], axis=0)
            ug = _mm([h], w_in, 3 * na_dim, 2 * sgu_dim, "gelu", F32)
            gated = _sgu(ug, sgu_ln_g[e], sgu_ln_b[e], sgu_w[e], sgu_b[e])
            z = _mm([att, gated], _Weight(mix_w_out, (e,)), 0, d, None, F32, residual=residual(zero_bias))
        else:
            o = l // 2
            w_in = _Weight(conv_w_in, (o,))
            glu = _dual(h, w_in, w_in, conv_dim, conv_dim, "glu", F32,
                        biases=conv_b_in[o].reshape(1, 2 * conv_dim))
            zc = _conv_mid(glu, conv_dw[o], conv_dw_b[o], conv_ln_g[o], conv_ln_b[o], tp, seq, dec_seq)
            z = _mm([zc], _Weight(conv_w_out, (o,)), 0, d, None, F32, residual=residual(conv_b_out[o]))
        x, h = _resid(None, z, None, mod, rows, l, 1, 1.0, alpha, ln_g[l, 1], ln_b[l, 1], (l, 2))

        a = _dual(h, _Weight(ffn_w_gate, (l, 1)), _Weight(ffn_w_up, (l, 1)), 0, d_ff, "swiglu", BF16,
                  pad_cols_to=_kacc_pad(d_ff))
        z = _mm_kacc(a, _Weight(ffn_w_down, (l, 1)), residual=_Residual(x, None, mod, rows, l, 2, 0.5, alpha))
        if l + 1 < depth:
            x, h = _resid(None, z, None, mod, rows, l, 2, 0.5, alpha, ln_g[l, 2], ln_b[l, 2], (l + 1, 0))
        else:
            args = (None, z, None, mod, rows, l, 2, 0.5, alpha, ln_g[l, 2], ln_b[l, 2], None)
            outs = (_resid(*args, row_range=(0, tp)), _resid(*args, row_range=(tp, n_dec * dec_seq)))

    y_prompt = outs[0].reshape(batch, seq, d)
    y_sample = outs[1].reshape(n_dec, dec_seq, d)
    new_cache_k = jnp.stack(new_k, axis=1)
    new_cache_v = jnp.stack(new_v, axis=1)
    return (y_prompt, y_sample, new_cache_k, new_cache_v)
```

```python
import functools
import math
from typing import NamedTuple

import jax
import jax.numpy as jnp
from jax import lax
from jax.experimental import pallas as pl
from jax.experimental.pallas import tpu as pltpu

F32 = jnp.float32
BF16 = jnp.bfloat16

GRID_W = 64
WIN_R = 8
WIN_C = 16
HEAD_DIM = 128
SGU_CHUNK = 128
N_MOD = 9
LN_EPS = 1e-5
NEG_INF = -1e30

V7X_VMEM_BYTES = 64 * 1024 * 1024
V7X_VMEM_LIMIT_CAP = 56 * 1024 * 1024
LANES = 128
SUBLANES = 8
MXU_COLS = 256


def _params(semantics, block_bytes, temp_bytes=0):
    need = 2 * block_bytes + temp_bytes
    limit = min(max(need + need // 4, 32 * 1024 * 1024), V7X_VMEM_LIMIT_CAP)
    return pltpu.CompilerParams(dimension_semantics=semantics, vmem_limit_bytes=int(limit))


def _nbytes(shape, dtype):
    return math.prod(shape) * jnp.dtype(dtype).itemsize


def _tile(dim, pref):
    if dim <= pref:
        return dim
    t = (pref // LANES) * LANES
    while t > LANES and dim % t:
        t -= LANES
    assert dim % t == 0, (dim, pref)
    return t


def _silu(x):
    return x * jax.nn.sigmoid(x)


def _gelu(x):
    return 0.5 * x * (1.0 + lax.erf(x * math.sqrt(0.5)))


def _layer_norm(z, g, b):
    mu = jnp.mean(z, axis=-1, keepdims=True)
    zc = z - mu
    var = jnp.mean(zc * zc, axis=-1, keepdims=True)
    return zc * lax.rsqrt(var + LN_EPS) * g + b


def _adaln_kernel(c_ref, w_ref, b_ref, o_ref):
    s = _silu(c_ref[...]).astype(BF16)
    o_ref[...] = jnp.dot(s, w_ref[...].astype(BF16), preferred_element_type=F32) + b_ref[...]


def _adaln(cond, ada_w, ada_b):
    n_layers, d, n = ada_w.shape
    rows = cond.shape[0]
    tn = _tile(n, 1024)
    blocks = _nbytes((rows, d), F32) + _nbytes((d, tn), F32) + 2 * _nbytes((rows, tn), F32)
    return pl.pallas_call(
        _adaln_kernel,
        grid=(n_layers, n // tn),
        in_specs=[
            pl.BlockSpec((rows, d), lambda l, j: (0, 0)),
            pl.BlockSpec((None, d, tn), lambda l, j: (l, 0, j)),
            pl.BlockSpec((None, 1, tn), lambda l, j: (l, 0, j)),
        ],
        out_specs=pl.BlockSpec((None, rows, tn), lambda l, j: (l, 0, j)),
        out_shape=jax.ShapeDtypeStruct((n_layers, rows, n), F32),
        compiler_params=_params(("arbitrary", "arbitrary"), blocks, _nbytes((d, tn), BF16)),
        name="adaln",
    )(cond, ada_w, ada_b.reshape(n_layers, 1, n))


class _Rows:
    def __init__(self, tp, dec_seq, n_dec):
        self.tp, self.dec_seq, self.n_dec = tp, dec_seq, n_dec
        self.total = tp + dec_seq * n_dec

    def group(self, row):
        return jnp.where(row < self.tp, 0, (row - self.tp) // self.dec_seq + 1)

    def check_block(self, tr):
        assert self.tp % tr == 0 and self.dec_seq % tr == 0, (self.tp, self.dec_seq, tr)


def _mod_spec(rows, layer, tr, d):
    return pl.BlockSpec((None, None, N_MOD, d), lambda i: (layer, rows.group(i * tr), 0, 0))


def _stacked_specs(tr, d, n_first):
    return [pl.BlockSpec((tr, d), lambda i: (jnp.minimum(i, n_first - 1), 0)),
            pl.BlockSpec((tr, d), lambda i: (jnp.maximum(i - n_first, 0), 0))]


def _stacked_pick(n_first, first_ref, second_ref):
    return jnp.where(pl.program_id(0) < n_first, first_ref[...], second_ref[...])


def _modulate_kernel(j, n_first, xa_ref, xb_ref, m_ref, o_ref):
    shift = m_ref[3 * j:3 * j + 1, :]
    scale = m_ref[3 * j + 1:3 * j + 2, :]
    x = _stacked_pick(n_first, xa_ref, xb_ref)
    o_ref[...] = (x * (1.0 + scale) + shift).astype(o_ref.dtype)


def _modulate(xa, xb, mod, rows, layer, j):
    d = xa.shape[1]
    t = xa.shape[0] + xb.shape[0]
    tr = 256
    rows.check_block(tr)
    n_first = xa.shape[0] // tr
    blocks = 2 * _nbytes((tr, d), F32) + _nbytes((tr, d), BF16) + _nbytes((N_MOD, d), F32)
    return pl.pallas_call(
        functools.partial(_modulate_kernel, j, n_first),
        grid=(t // tr,),
        in_specs=_stacked_specs(tr, d, n_first) + [_mod_spec(rows, layer, tr, d)],
        out_specs=pl.BlockSpec((tr, d), lambda i: (i, 0)),
        out_shape=jax.ShapeDtypeStruct((t, d), BF16),
        compiler_params=_params(("arbitrary",), blocks),
        name="modulate",
    )(xa, xb, mod)


_LN_ROWS = 8


def _combine(alpha, weight, gate, x, y, y_bias):
    return alpha * x + (weight * gate) * (y + y_bias)


def _resid_kernel(j, weight, alpha, j_next, x_mode, *refs):
    refs = list(refs)
    if x_mode is None:
        x_refs = []
    elif x_mode == "single":
        x_refs = [refs.pop(0)]
    else:
        x_refs = [refs.pop(0), refs.pop(0)]
    y_ref = refs.pop(0)
    if x_mode is not None:
        yb_ref, m_ref = refs.pop(0), refs.pop(0)
    g_ref, b_ref = refs.pop(0), refs.pop(0)
    if j_next is not None:
        mn_ref = refs.pop(0)
        xo_ref, ho_ref, rows_ref = refs
    else:
        xo_ref, rows_ref = refs

    d = y_ref.shape[1]
    vectors = {"g": g_ref[...], "b": b_ref[...]}
    if x_mode is not None:
        vectors["wg"] = weight * m_ref[3 * j + 2:3 * j + 3, :]
        vectors["yb"] = yb_ref[...]
    if j_next is not None:
        vectors["scale1"] = 1.0 + mn_ref[3 * j_next + 1:3 * j_next + 2, :]
        vectors["shift"] = mn_ref[3 * j_next:3 * j_next + 1, :]
    slot = {name: i for i, name in enumerate(vectors)}
    for name, v in vectors.items():
        rows_ref[slot[name]] = jnp.broadcast_to(v, (_LN_ROWS, d))

    def slab(c, carry):
        r = pl.ds(pl.multiple_of(c * _LN_ROWS, _LN_ROWS), _LN_ROWS)
        if x_mode is None:
            z = y_ref[r, :]
        else:
            if x_mode == "single":
                x = x_refs[0][r, :]
            else:
                x = jnp.where(pl.program_id(0) < x_mode[1], x_refs[0][r, :], x_refs[1][r, :])
            z = alpha * x + rows_ref[slot["wg"]] * (y_ref[r, :] + rows_ref[slot["yb"]])
        xn = _layer_norm(z, rows_ref[slot["g"]], rows_ref[slot["b"]])
        xo_ref[r, :] = xn
        if j_next is not None:
            ho_ref[r, :] = (xn * rows_ref[slot["scale1"]] + rows_ref[slot["shift"]]).astype(ho_ref.dtype)
        return carry

    lax.fori_loop(0, y_ref.shape[0] // _LN_ROWS, slab, 0, unroll=8)


def _resid(x, y, y_bias, mod, rows, layer, j, weight, alpha, ln_g, ln_b, nxt, row_range=None):
    t, d = y.shape
    tr = 256
    rows.check_block(tr)
    start, count = (0, t) if row_range is None else row_range
    assert start % tr == 0 and count % tr == 0
    off = start // tr
    row_spec = pl.BlockSpec((tr, d), lambda i: (i + off, 0))
    vec_spec = pl.BlockSpec((1, d), lambda i: (0, 0))

    def mod_spec(layer_):
        return pl.BlockSpec((None, None, N_MOD, d), lambda i: (layer_, rows.group((i + off) * tr), 0, 0))

    if x is None:
        assert y_bias is None
        x_mode, x_specs, x_args = None, [], []
    elif isinstance(x, tuple):
        assert row_range is None
        n_first = x[0].shape[0] // tr
        x_mode, x_specs, x_args = ("stacked", n_first), _stacked_specs(tr, d, n_first), list(x)
    else:
        x_mode, x_specs, x_args = "single", [row_spec], [x]
    in_specs = x_specs + [row_spec]
    args = x_args + [y]
    if x is not None:
        in_specs += [vec_spec, mod_spec(layer)]
        args += [y_bias.reshape(1, d), mod]
    in_specs += [vec_spec, vec_spec]
    args += [ln_g.reshape(1, d), ln_b.reshape(1, d)]
    out_spec = pl.BlockSpec((tr, d), lambda i: (i, 0))
    n_vectors = 2 + (2 if x is not None else 0) + (2 if nxt is not None else 0)
    blocks = (2 + len(x_args)) * _nbytes((tr, d), F32) + 2 * _nbytes((N_MOD, d), F32)
    if nxt is None:
        out_specs, out_shape, j_next = out_spec, jax.ShapeDtypeStruct((count, d), F32), None
    else:
        in_specs.append(mod_spec(nxt[0]))
        args.append(mod)
        out_specs = [out_spec, out_spec]
        out_shape = [jax.ShapeDtypeStruct((count, d), F32), jax.ShapeDtypeStruct((count, d), BF16)]
        j_next = nxt[1]
        blocks += _nbytes((tr, d), BF16)
    return pl.pallas_call(
        functools.partial(_resid_kernel, j, weight, alpha, j_next, x_mode),
        grid=(count // tr,),
        in_specs=in_specs,
        out_specs=out_specs,
        out_shape=out_shape,
        scratch_shapes=[pltpu.VMEM((n_vectors, _LN_ROWS, d), F32)],
        compiler_params=_params(("arbitrary",), blocks, _nbytes((n_vectors, _LN_ROWS, d), F32)),
        name="resid_ln",
    )(*args)


class _Weight(NamedTuple):
    arr: jax.Array
    lead: tuple = ()

    @property
    def shape(self):
        return self.arr.shape[len(self.lead):]

    def spec(self, block, index_fn):
        lead = self.lead
        return pl.BlockSpec((None,) * len(lead) + tuple(block), lambda *g: lead + tuple(index_fn(*g)))


def _dual_kernel(act, has_bias, n_valid, n_total, h_ref, w1_ref, w2_ref, *rest):
    o_ref = rest[-1]

    def compute():
        h = h_ref[...]
        a = jnp.dot(h, w1_ref[...].astype(BF16), preferred_element_type=F32)
        b = jnp.dot(h, w2_ref[...].astype(BF16), preferred_element_type=F32)
        if has_bias:
            a = a + rest[0][...]
            b = b + rest[1][...]
        out = _silu(a) * b if act == "swiglu" else a * jax.nn.sigmoid(b)
        o_ref[...] = out.astype(o_ref.dtype)

    if n_valid == n_total:
        compute()
    else:
        pl.when(pl.program_id(1) < n_valid)(compute)

        @pl.when(pl.program_id(1) >= n_valid)
        def _():
            o_ref[...] = jnp.zeros_like(o_ref)


def _dual(h, w1, w2, col2_off, n_cols, act, out_dtype, biases=None, pad_cols_to=None):
    t, k = h.shape
    tm = _tile(t, 2048)
    tn = MXU_COLS
    n_out = n_cols if pad_cols_to is None else pad_cols_to
    assert n_cols % tn == 0 and col2_off % tn == 0 and n_out % tn == 0 and n_out >= n_cols
    off2 = col2_off // tn
    n_valid, n_total = n_cols // tn, n_out // tn

    def col(n):
        return jnp.minimum(n, n_valid - 1)

    in_specs = [
        pl.BlockSpec((tm, k), lambda m, n: (m, 0), pipeline_mode=pl.Buffered(1)),
        w1.spec((k, tn), lambda m, n: (0, col(n))),
        w2.spec((k, tn), lambda m, n: (0, col(n) + off2)),
    ]
    args = [h, w1.arr, w2.arr]
    if biases is not None:
        in_specs += [pl.BlockSpec((1, tn), lambda m, n: (0, col(n))),
                     pl.BlockSpec((1, tn), lambda m, n: (0, col(n) + off2))]
        args += [biases, biases]
    blocks = 2 * _nbytes((k, tn), F32) + _nbytes((tm, tn), out_dtype)
    temps = _nbytes((tm, k), BF16) + 2 * _nbytes((k, tn), BF16) + 4 * _nbytes((tm, tn), F32)
    return pl.pallas_call(
        functools.partial(_dual_kernel, act, biases is not None, n_valid, n_total),
        grid=(t // tm, n_total),
        in_specs=in_specs,
        out_specs=pl.BlockSpec((tm, tn), lambda m, n: (m, n)),
        out_shape=jax.ShapeDtypeStruct((t, n_out), out_dtype),
        compiler_params=_params(("arbitrary", "arbitrary"), blocks, temps),
        name="dual_" + act,
    )(*args)


class _Residual(NamedTuple):
    x: jax.Array
    y_bias: jax.Array
    mod: jax.Array
    rows: _Rows
    layer: int
    j: int
    weight: float
    alpha: float


def _mm_kernel(n_pairs, act, residual, *refs):
    o_ref = refs[-1]
    acc = None
    for x_ref, w_ref in zip(refs[:n_pairs], refs[n_pairs:2 * n_pairs]):
        p = jnp.dot(x_ref[...], w_ref[...].astype(BF16), preferred_element_type=F32)
        acc = p if acc is None else acc + p
    if act == "gelu":
        acc = _gelu(acc)
    if residual is not None:
        j, weight, alpha = residual
        x_ref, yb_ref, m_ref = refs[2 * n_pairs:2 * n_pairs + 3]
        acc = _combine(alpha, weight, m_ref[3 * j + 2:3 * j + 3, :], x_ref[...], acc, yb_ref[...])
    o_ref[...] = acc.astype(o_ref.dtype)


def _mm(xs, w, col_off, n_cols, act, out_dtype, residual=None):
    t, k = xs[0].shape
    tm = _tile(t, 1024) if residual is None else _tile(math.gcd(residual.rows.tp, residual.rows.dec_seq), 1024)
    tn = _tile(math.gcd(n_cols, col_off), 512)
    assert all(x.shape == (t, k) for x in xs) and t % tm == 0
    offn = col_off // tn
    in_specs = [pl.BlockSpec((tm, k), lambda m, n: (m, 0)) for _ in xs]
    in_specs += [w.spec((k, tn), functools.partial(lambda i, m, n: (i, n + offn), i)) for i in range(len(xs))]
    args = [*xs, *([w.arr] * len(xs))]
    blocks = len(xs) * (_nbytes((tm, k), BF16) + _nbytes((k, tn), F32)) + _nbytes((tm, tn), out_dtype)
    temps = len(xs) * _nbytes((k, tn), BF16) + 3 * _nbytes((tm, tn), F32)
    consts = None
    if residual is not None:
        r = residual
        assert act is None and col_off == 0 and r.x.shape == (t, n_cols)
        in_specs += [
            pl.BlockSpec((tm, tn), lambda m, n: (m, n)),
            pl.BlockSpec((1, tn), lambda m, n: (0, n)),
            pl.BlockSpec((None, None, N_MOD, tn), lambda m, n: (r.layer, r.rows.group(m * tm), 0, n)),
        ]
        args += [r.x, r.y_bias.reshape(1, n_cols), r.mod]
        blocks += _nbytes((tm, tn), F32) + _nbytes((N_MOD + 1, tn), F32)
        consts = (r.j, r.weight, r.alpha)
    return pl.pallas_call(
        functools.partial(_mm_kernel, len(xs), act, consts),
        grid=(t // tm, n_cols // tn),
        in_specs=in_specs,
        out_specs=pl.BlockSpec((tm, tn), lambda m, n: (m, n)),
        out_shape=jax.ShapeDtypeStruct((t, n_cols), out_dtype),
        compiler_params=_params(("arbitrary", "arbitrary"), blocks, temps),
        name="mm_" + str(act),
    )(*args)


_KACC_SUB = 256
_KACC_NSUB = 4


def _kacc_pad(k):
    step = _KACC_SUB * _KACC_NSUB
    return -(-k // step) * step


def _kacc_kernel(residual, last_used, a_ref, *refs):
    w_refs, refs = refs[:_KACC_NSUB], refs[_KACC_NSUB:]
    kk, last = pl.program_id(2), pl.num_programs(2) - 1

    def product(used=_KACC_NSUB):
        w = jnp.concatenate([w_ref[...].astype(BF16) for w_ref in w_refs[:used]], axis=0)
        return jnp.dot(a_ref[:, :used * _KACC_SUB], w, preferred_element_type=F32)

    if residual is None:
        (o_ref,) = refs

        @pl.when(kk == 0)
        def _():
            o_ref[...] = product()

        @pl.when((kk != 0) & (kk != last))
        def _():
            o_ref[...] += product()

        @pl.when(kk == last)
        def _():
            o_ref[...] += product(last_used)
        return

    j, weight, alpha, part_rows = residual
    x_hbm, *m_refs, o_ref, x_buf, sem = refs
    tm, tn = o_ref.shape

    def x_copy():
        rows = pl.ds(pl.multiple_of(pl.program_id(0) * tm, tm), tm)
        cols = pl.ds(pl.multiple_of(pl.program_id(1) * tn, tn), tn)
        return pltpu.make_async_copy(x_hbm.at[rows, cols], x_buf, sem)

    @pl.when(kk == 0)
    def _():
        x_copy().start()
        o_ref[...] = product()

    @pl.when((kk != 0) & (kk != last))
    def _():
        o_ref[...] += product()

    @pl.when(kk == last)
    def _():
        y = o_ref[...] + product(last_used)
        x_copy().wait()
        for p, m_ref in enumerate(m_refs):
            rows = slice(p * part_rows, (p + 1) * part_rows)
            gate = m_ref[3 * j + 2:3 * j + 3, :]
            o_ref[rows, :] = alpha * x_buf[rows, :] + (weight * gate) * y[rows, :]


def _mm_kacc(a, w, residual=None):
    t, k_pad = a.shape
    k, n = w.shape
    assert k % _KACC_SUB == 0 and k_pad == _kacc_pad(k)
    n_sub = k // _KACC_SUB
    tn = _tile(n, 1024)
    tk = _KACC_SUB * _KACC_NSUB
    steps = k_pad // tk
    last_used = n_sub - (steps - 1) * _KACC_NSUB
    assert steps >= 2 and 1 <= last_used <= _KACC_NSUB
    if residual is None:
        tm = _tile(t, 2048)
    else:
        r = residual
        part = _tile(math.gcd(r.rows.tp, r.rows.dec_seq), 1024)
        tm = 2 * part if t % (2 * part) == 0 else part
        assert r.y_bias is None and r.x.shape == (t, n) and t % tm == 0
    w_specs = [
        w.spec((_KACC_SUB, tn), functools.partial(
            lambda s, m, j, kk: (jnp.minimum(kk * _KACC_NSUB + s, n_sub - 1), j), s))
        for s in range(_KACC_NSUB)
    ]
    in_specs = [pl.BlockSpec((tm, tk), lambda m, j, kk: (m, kk))] + w_specs
    args = [a, *([w.arr] * _KACC_NSUB)]
    blocks = _nbytes((tm, tk), BF16) + _nbytes((tk, tn), F32) + _nbytes((tm, tn), F32)
    temps = _nbytes((tk, tn), BF16) + _nbytes((tm, tn), F32)
    consts, scratch = None, []
    if residual is not None:
        n_parts = tm // part
        in_specs.append(pl.BlockSpec(memory_space=pl.ANY))
        in_specs += [
            pl.BlockSpec((None, None, N_MOD, tn), functools.partial(
                lambda p, m, j, kk: (r.layer, r.rows.group(m * tm + p * part), 0, j), p))
            for p in range(n_parts)
        ]
        args += [r.x, *([r.mod] * n_parts)]
        scratch = [pltpu.VMEM((tm, tn), F32), pltpu.SemaphoreType.DMA(())]
        temps += _nbytes((tm, tn), F32)
        consts = (r.j, r.weight, r.alpha, part)
    return pl.pallas_call(
        functools.partial(_kacc_kernel, consts, last_used),
        grid=(t // tm, n // tn, steps),
        in_specs=in_specs,
        out_specs=pl.BlockSpec((tm, tn), lambda m, j, kk: (m, j)),
        out_shape=jax.ShapeDtypeStruct((t, n), F32),
        scratch_shapes=scratch,
        compiler_params=_params(("arbitrary", "arbitrary", "arbitrary"), blocks, temps),
        name="mm_kacc",
    )(*args)


def _heads_kernel(seqs, seq_len, x_ref, w_ref, o_ref):
    acc = jnp.dot(x_ref[...], w_ref[...].astype(BF16), preferred_element_type=F32)
    for s in range(seqs):
        for hh in range(o_ref.shape[1]):
            o_ref[s, hh] = acc[s * seq_len:(s + 1) * seq_len, hh * HEAD_DIM:(hh + 1) * HEAD_DIM]


def _mm_heads(h, w, col_off, n_cols, row_start, n_seq, seq_len):
    k = h.shape[1]
    tm = max(seq_len, _tile(n_seq * seq_len, 1024))
    assert tm % seq_len == 0 and (n_seq * seq_len) % tm == 0 and row_start % tm == 0
    seqs = tm // seq_len
    tn = _tile(math.gcd(n_cols, col_off), 2 * MXU_COLS)
    hpb = tn // HEAD_DIM
    offm, offn = row_start // tm, col_off // tn
    blocks = _nbytes((tm, k), BF16) + _nbytes((k, tn), F32) + _nbytes((tm, tn), F32)
    temps = _nbytes((k, tn), BF16) + 2 * _nbytes((tm, tn), F32)
    return pl.pallas_call(
        functools.partial(_heads_kernel, seqs, seq_len),
        grid=(n_seq // seqs, n_cols // tn),
        in_specs=[pl.BlockSpec((tm, k), lambda m, n: (m + offm, 0)), w.spec((k, tn), lambda m, n: (0, n + offn))],
        out_specs=pl.BlockSpec((seqs, hpb, seq_len, HEAD_DIM), lambda m, n: (m, n, 0, 0)),
        out_shape=jax.ShapeDtypeStruct((n_seq, n_cols // HEAD_DIM, seq_len, HEAD_DIM), F32),
        compiler_params=_params(("arbitrary", "arbitrary"), blocks, temps),
        name="mm_heads",
    )(h, w.arr)


_NT = (((1,), (1,)), ((), ()))


def _ctx_attn_kernel(heads, q_ref, k_ref, v_ref, o_ref):
    scale = HEAD_DIM ** -0.5
    for h in range(heads):
        q = q_ref[h].astype(BF16)
        k = k_ref[h].astype(BF16)
        v = v_ref[h].astype(BF16)
        s = lax.dot_general(q, k, _NT, preferred_element_type=F32) * scale
        p = jnp.exp(s - jnp.max(s, axis=-1, keepdims=True))
        l = jnp.sum(p, axis=-1, keepdims=True)
        o = jnp.dot(p.astype(BF16), v, preferred_element_type=F32) / l
        o_ref[:, h * HEAD_DIM:(h + 1) * HEAD_DIM] = o.astype(o_ref.dtype)


def _ctx_attention(q, k, v):
    b, nh, s, hd = q.shape
    hb = math.gcd(nh, 8)
    spec = pl.BlockSpec((None, hb, s, hd), lambda i, g: (i, g, 0, 0))
    blocks = 3 * _nbytes((hb, s, hd), F32) + _nbytes((s, hb * hd), BF16)
    return pl.pallas_call(
        functools.partial(_ctx_attn_kernel, hb),
        grid=(b, nh // hb),
        in_specs=[spec, spec, spec],
        out_specs=pl.BlockSpec((s, hb * hd), lambda i, g: (i, g)),
        out_shape=jax.ShapeDtypeStruct((b * s, nh * hd), BF16),
        compiler_params=_params(("arbitrary", "arbitrary"), blocks, 8 * _nbytes((s, s), F32)),
        name="ctx_attention",
    )(q, k, v)


def _window_start(i, n, w):
    return min(max(i - w // 2, 0), n - w)


_N_DR = 2 * WIN_R - 1
_N_DC = 2 * WIN_C - 1
_PAIR = LANES // GRID_W


_NA_GROUP = 4
_NA_UNION = 12


class _NaPlan(NamedTuple):
    groups: tuple
    pad_left: int
    n_blocks: int


def _na_plan(rows):
    assert rows % _NA_GROUP == 0 and rows >= _NA_UNION and GRID_W * _PAIR == LANES and _NA_UNION % _PAIR == 0
    groups = []
    for r0 in range(0, rows, _NA_GROUP):
        base = min(_window_start(r0, rows, WIN_R), rows - _NA_UNION)
        bands = []
        for r in range(r0, r0 + _NA_GROUP):
            rs = _window_start(r, rows, WIN_R)
            assert base <= rs and rs + WIN_R <= base + _NA_UNION
            bands.append((rs - base, base - r + WIN_R - 1))
        groups.append((r0, base, tuple(bands)))
    firsts = [first for _, _, bands in groups for _, first in bands]
    pad_left = max(0, -min(firsts))
    pad_right = max(0, max(firsts) + _NA_UNION - _N_DR)
    return _NaPlan(tuple(groups), pad_left, pad_left + _N_DR + pad_right)


def _na_strip_slots(plan):
    return (plan.n_blocks - 1 + _PAIR - 1) // _PAIR


def _na_build_bias(head, plan, rpb_ref, strips_ref, bias_ref):
    lane8 = lax.broadcasted_iota(jnp.int32, (SUBLANES, LANES), 1)
    tap = jnp.clip(lane8 - (GRID_W - 1), -(WIN_C - 1), WIN_C - 1) + (WIN_C - 1)
    table = head * (_N_DR * _N_DC)
    lower, upper = [], []
    for d in range(_N_DR):
        def pick(j, e, d=d):
            return jnp.where(tap == j, rpb_ref[table + d * _N_DC + j], e)

        e = lax.fori_loop(0, _N_DC, pick, jnp.zeros((SUBLANES, LANES), F32))
        e = jnp.broadcast_to(e[0:1, :], (GRID_W, LANES))
        lower.append(pltpu.roll(e, LANES - (GRID_W - 1), 1, stride=1, stride_axis=0))
        upper.append(pltpu.roll(e, 1, 1, stride=1, stride_axis=0))
    zero = jnp.zeros((GRID_W, LANES), F32)
    in_upper = lax.broadcasted_iota(jnp.int32, (GRID_W, LANES), 1) >= GRID_W

    def block(b, halves):
        d = b - plan.pad_left
        return halves[d] if 0 <= d < _N_DR else zero

    for q in range(plan.n_blocks - 1):
        strips_ref[q % _PAIR, :, (q // _PAIR) * LANES:(q // _PAIR + 1) * LANES] = jnp.where(
            in_upper, block(q + 1, upper), block(q, lower))

    shape = (_NA_GROUP * GRID_W, _NA_UNION * GRID_W)
    row = lax.broadcasted_iota(jnp.int32, shape, 0)
    col = lax.broadcasted_iota(jnp.int32, shape, 1)
    band, cq = row // GRID_W, row % GRID_W
    urow, ck = col // GRID_W, col % GRID_W
    cs = jnp.clip(cq - WIN_C // 2, 0, GRID_W - WIN_C)
    col_ok = (ck >= cs) & (ck < cs + WIN_C)
    for g, (_, _, bands) in enumerate(plan.groups):
        lo = jnp.zeros(shape, jnp.int32)
        pieces = []
        for j, (lo_j, first) in enumerate(bands):
            lo = jnp.where(band == j, lo_j, lo)
            s = first + plan.pad_left
            pieces.append(strips_ref[s % _PAIR, :, (s // _PAIR) * LANES:(s // _PAIR) * LANES + shape[1]])
        ok = col_ok & (urow >= lo) & (urow < lo + WIN_R)
        bias_ref[g] = jnp.where(ok, jnp.concatenate(pieces, axis=0), NEG_INF)


def _na_kernel(plan, rpb_ref, q_ref, k_ref, v_ref, ck_ref, cv_ref, o_ref, strips_ref, bias_ref):
    @pl.when(pl.program_id(1) == 0)
    def _():
        _na_build_bias(pl.program_id(0), plan, rpb_ref, strips_ref, bias_ref)

    scale = HEAD_DIM ** -0.5
    kb = k_ref[...].astype(BF16)
    vb = v_ref[...].astype(BF16)
    ckb = ck_ref[...].astype(BF16)
    cvb = cv_ref[...].astype(BF16)
    for g, (r0, base, _) in enumerate(plan.groups):
        queries = slice(r0 * GRID_W, (r0 + _NA_GROUP) * GRID_W)
        keys = slice(base * GRID_W, (base + _NA_UNION) * GRID_W)
        q = q_ref[queries, :].astype(BF16)
        s_loc = lax.dot_general(q, kb[keys], _NT, preferred_element_type=F32) * scale + bias_ref[g]
        s_ctx = lax.dot_general(q, ckb, _NT, preferred_element_type=F32) * scale
        m = jnp.maximum(jnp.max(s_loc, axis=-1, keepdims=True), jnp.max(s_ctx, axis=-1, keepdims=True))
        p_loc = jnp.exp(s_loc - m)
        p_ctx = jnp.exp(s_ctx - m)
        l = jnp.sum(p_loc, axis=-1, keepdims=True) + jnp.sum(p_ctx, axis=-1, keepdims=True)
        o = jnp.dot(p_loc.astype(BF16), vb[keys], preferred_element_type=F32)
        o = o + jnp.dot(p_ctx.astype(BF16), cvb, preferred_element_type=F32)
        o_ref[queries, :] = (o / l).astype(o_ref.dtype)


def _na_attention(qkv, cache_k, cache_v, layer_e, rpb):
    b, nh3, n, hd = qkv.shape
    nh = nh3 // 3
    past = cache_k.shape[3]
    plan = _na_plan(n // GRID_W)
    assert rpb.shape == (nh, _N_DR, _N_DC)
    q_spec, k_spec, v_spec = (
        pl.BlockSpec((None, None, n, hd), functools.partial(lambda part, h, i: (i, part * nh + h, 0, 0), part))
        for part in range(3))
    ctx_spec = pl.BlockSpec((None, None, None, past, hd), lambda h, i: (i, layer_e, h, 0, 0))
    strip_shape = (_PAIR, GRID_W, _na_strip_slots(plan) * LANES)
    bias_shape = (len(plan.groups), _NA_GROUP * GRID_W, _NA_UNION * GRID_W)
    blocks = 3 * _nbytes((n, hd), F32) + 2 * _nbytes((past, hd), F32) + _nbytes((n, hd), BF16)
    scratch = _nbytes(strip_shape, F32) + _nbytes(bias_shape, F32)
    return pl.pallas_call(
        functools.partial(_na_kernel, plan),
        grid=(nh, b),
        in_specs=[pl.BlockSpec(memory_space=pltpu.SMEM), q_spec, k_spec, v_spec, ctx_spec, ctx_spec],
        out_specs=pl.BlockSpec((n, hd), lambda h, i: (i, h)),
        out_shape=jax.ShapeDtypeStruct((b * n, nh * hd), BF16),
        scratch_shapes=[pltpu.VMEM(strip_shape, F32), pltpu.VMEM(bias_shape, F32)],
        compiler_params=_params(("arbitrary", "arbitrary"), blocks, scratch + 8 * _nbytes(bias_shape[1:], F32)),
        name="na_attention",
    )(rpb.reshape(-1), qkv, qkv, qkv, cache_k, cache_v)


def _sgu_kernel(chunks, groups, u_ref, vg_ref, g_ref, b_ref, ws_ref, bs_ref, o_ref):
    gc = u_ref.shape[1] // groups
    for c in range(chunks):
        rows = pl.ds(c * SGU_CHUNK, SGU_CHUNK)
        vn = _layer_norm(vg_ref[rows, :], g_ref[...], b_ref[...]).astype(BF16)
        for g in range(groups):
            cols = slice(g * gc, (g + 1) * gc)
            s = jnp.dot(ws_ref[g].astype(BF16), vn[:, cols], preferred_element_type=F32) + bs_ref[g]
            o_ref[rows, cols] = (u_ref[rows, cols] * s).astype(o_ref.dtype)


def _sgu(ug, ln_g, ln_b, w_s, b_s):
    t, c2 = ug.shape
    c = c2 // 2
    groups = w_s.shape[0]
    tr = 2 * SGU_CHUNK
    assert t % tr == 0 and w_s.shape[1:] == (SGU_CHUNK, SGU_CHUNK)
    vec = pl.BlockSpec((1, c), lambda i: (0, 0))
    blocks = (2 * _nbytes((tr, c), F32) + _nbytes((tr, c), BF16) + _nbytes(w_s.shape, F32)
              + _nbytes((groups, SGU_CHUNK, LANES), F32))
    return pl.pallas_call(
        functools.partial(_sgu_kernel, tr // SGU_CHUNK, groups),
        grid=(t // tr,),
        in_specs=[
            pl.BlockSpec((tr, c), lambda i: (i, 0)),
            pl.BlockSpec((tr, c), lambda i: (i, 1)),
            vec, vec,
            pl.BlockSpec(w_s.shape, lambda i: (0, 0, 0)),
            pl.BlockSpec((groups, SGU_CHUNK, 1), lambda i: (0, 0, 0)),
        ],
        out_specs=pl.BlockSpec((tr, c), lambda i: (i, 0)),
        out_shape=jax.ShapeDtypeStruct((t, c), BF16),
        compiler_params=_params(("arbitrary",), blocks, 4 * _nbytes((SGU_CHUNK, c), F32)),
        name="sgu",
    )(ug, ug, ln_g.reshape(1, c), ln_b.reshape(1, c), w_s, b_s.reshape(groups, SGU_CHUNK, 1))


_CONV_TILE = 256
_CONV_HALO = 16
_CONV_ROWS = 64


def _conv_kernel(tp, tp_seq, dec_seq, taps, prev_ref, cur_ref, next_ref, dw_ref, dwb_ref, g_ref, b_ref, o_ref,
                 zp_ref, y_ref):
    row0 = pl.program_id(0) * _CONV_TILE
    in_ctx = row0 < tp
    seq_len = jnp.where(in_ctx, tp_seq, dec_seq)
    pos = jnp.where(in_ctx, row0, row0 - tp) % seq_len
    has_prev = pos > 0
    has_next = pos + _CONV_TILE < seq_len
    n_blk = cur_ref.shape[1] // LANES
    for cb in range(n_blk):
        cols = slice(cb * LANES, (cb + 1) * LANES)
        zp_ref[cb, 0:_CONV_HALO, :] = jnp.where(has_prev, prev_ref[:, cols], 0.0)
        zp_ref[cb, _CONV_HALO:_CONV_HALO + _CONV_TILE, :] = cur_ref[:, cols]
        zp_ref[cb, _CONV_HALO + _CONV_TILE:, :] = jnp.where(has_next, next_ref[:, cols], 0.0)
    half = taps // 2

    def col_block(cb, carry):
        for rb in range(_CONV_TILE // _CONV_ROWS):
            acc = jnp.broadcast_to(dwb_ref[cb], (_CONV_ROWS, LANES))
            for kk in range(taps):
                start = _CONV_HALO + rb * _CONV_ROWS + kk - half
                acc = acc + zp_ref[cb, pl.ds(start, _CONV_ROWS), :] * dw_ref[cb, kk:kk + 1, :]
            y_ref[cb, rb * _CONV_ROWS:(rb + 1) * _CONV_ROWS, :] = acc
        return carry

    lax.fori_loop(0, n_blk, col_block, 0)

    d = n_blk * LANES
    part = y_ref[0]
    for cb in range(1, n_blk):
        part = part + y_ref[cb]
    mu = jnp.sum(part, axis=-1, keepdims=True) / d
    part = jnp.square(y_ref[0] - mu)
    for cb in range(1, n_blk):
        part = part + jnp.square(y_ref[cb] - mu)
    rstd = lax.rsqrt(jnp.sum(part, axis=-1, keepdims=True) / d + LN_EPS)
    for cb in range(n_blk):
        cols = slice(cb * LANES, (cb + 1) * LANES)
        o_ref[:, cols] = _silu((y_ref[cb] - mu) * rstd * g_ref[:, cols] + b_ref[:, cols]).astype(o_ref.dtype)


def _conv_mid(z, dw, dw_b, ln_g, ln_b, tp, tp_seq, dec_seq):
    t, c = z.shape
    taps = dw.shape[0]
    assert taps // 2 < _CONV_HALO and tp_seq % _CONV_TILE == 0 and dec_seq % _CONV_TILE == 0 and c % LANES == 0
    n_blk = c // LANES
    hb = _CONV_TILE // _CONV_HALO
    last_halo = t // _CONV_HALO - 1
    padded = _CONV_TILE + 2 * _CONV_HALO
    vec = pl.BlockSpec((1, c), lambda i: (0, 0))
    dw_blocks = dw.reshape(taps, n_blk, LANES).transpose(1, 0, 2)
    blocks = (_nbytes((padded, c), F32) + _nbytes((n_blk, 32, LANES), F32) + _nbytes((_CONV_TILE, c), BF16))
    scratch = _nbytes((padded + _CONV_TILE, c), F32)
    return pl.pallas_call(
        functools.partial(_conv_kernel, tp, tp_seq, dec_seq, taps),
        grid=(t // _CONV_TILE,),
        in_specs=[
            pl.BlockSpec((_CONV_HALO, c), lambda i: (jnp.maximum(i * hb - 1, 0), 0)),
            pl.BlockSpec((_CONV_TILE, c), lambda i: (i, 0)),
            pl.BlockSpec((_CONV_HALO, c), lambda i: (jnp.minimum((i + 1) * hb, last_halo), 0)),
            pl.BlockSpec((n_blk, taps, LANES), lambda i: (0, 0, 0)),
            pl.BlockSpec((n_blk, 1, LANES), lambda i: (0, 0, 0)),
            vec, vec,
        ],
        out_specs=pl.BlockSpec((_CONV_TILE, c), lambda i: (i, 0)),
        out_shape=jax.ShapeDtypeStruct((t, c), BF16),
        scratch_shapes=[pltpu.VMEM((n_blk, padded, LANES), F32), pltpu.VMEM((n_blk, _CONV_TILE, LANES), F32)],
        compiler_params=_params(("arbitrary",), blocks, scratch + 2 * _nbytes((_CONV_TILE, c), F32)),
        name="conv_mid",
    )(z, z, z, dw_blocks, dw_b.reshape(n_blk, 1, LANES), ln_g.reshape(1, c), ln_b.reshape(1, c))


def kernel(x_prompt, x_sample, cache_k, cache_v, c, c_ctx, ada_w, ada_b, ln_g, ln_b, ffn_w_gate, ffn_w_up, ffn_w_down, mix_w_in, na_rpb, sgu_ln_g, sgu_ln_b, sgu_w, sgu_b, mix_w_out, conv_w_in, conv_b_in, conv_dw, conv_dw_b, conv_ln_g, conv_ln_b, conv_w_out, conv_b_out):
    batch, seq, d = x_prompt.shape
    n_dec, dec_seq, _ = x_sample.shape
    depth = ada_w.shape[0]
    n_heads = na_rpb.shape[1]
    na_dim = n_heads * HEAD_DIM
    sgu_dim = sgu_ln_g.shape[1]
    conv_dim = conv_dw.shape[2]
    d_ff = ffn_w_gate.shape[3]
    alpha = (2 * depth) ** 0.25
    tp = batch * seq
    rows = _Rows(tp, dec_seq, n_dec)
    n_groups = 1 + n_dec
    assert N_MOD * d == ada_w.shape[2] and mix_w_in.shape[2] == 3 * na_dim + 2 * sgu_dim

    cond = jnp.concatenate([c_ctx[None, :], c, jnp.zeros((SUBLANES - n_groups, d), F32)], axis=0)
    mod = _adaln(cond, ada_w, ada_b)[:, :n_groups].reshape(depth, n_groups, N_MOD, d)

    x = (x_prompt.reshape(tp, d), x_sample.reshape(n_dec * dec_seq, d))
    h = _modulate(x[0], x[1], mod, rows, 0, 0)
    zero_bias = jnp.zeros((d,), F32)
    new_k, new_v = [], []
    outs = None

    for l in range(depth):
        a = _dual(h, _Weight(ffn_w_gate, (l, 0)), _Weight(ffn_w_up, (l, 0)), 0, d_ff, "swiglu", BF16,
                  pad_cols_to=_kacc_pad(d_ff))
        if isinstance(x, tuple):
            y = _mm_kacc(a, _Weight(ffn_w_down, (l, 0)))
            x, h = _resid(x, y, zero_bias, mod, rows, l, 0, 0.5, alpha, ln_g[l, 0], ln_b[l, 0], (l, 1))
        else:
            z = _mm_kacc(a, _Weight(ffn_w_down, (l, 0)), residual=_Residual(x, None, mod, rows, l, 0, 0.5, alpha))
            x, h = _resid(None, z, None, mod, rows, l, 0, 0.5, alpha, ln_g[l, 0], ln_b[l, 0], (l, 1))

        def residual(y_bias, x=x, l=l):
            return _Residual(x, y_bias, mod, rows, l, 1, 1.0, alpha)

        if l % 2 == 0:
            e = l // 2
            w_in = _Weight(mix_w_in, (e,))
            qp, kp, vp = (_mm_heads(h, w_in, i * na_dim, na_dim, 0, batch, seq) for i in range(3))
            qkv_s = _mm_heads(h, w_in, 0, 3 * na_dim, tp, n_dec, dec_seq)
            new_k.append(kp)
            new_v.append(vp)
            att = jnp.concatenate(
                [_ctx_attention(qp, kp, vp), _na_attention(qkv_s, cache_k, cache_v, e, na_rpb[e])], axis=0)
            ug = _mm([h], w_in, 3 * na_dim, 2 * sgu_dim, "gelu", F32)
            gated = _sgu(ug, sgu_ln_g[e], sgu_ln_b[e], sgu_w[e], sgu_b[e])
            z = _mm([att, gated], _Weight(mix_w_out, (e,)), 0, d, None, F32, residual=residual(zero_bias))
        else:
            o = l // 2
            w_in = _Weight(conv_w_in, (o,))
            glu = _dual(h, w_in, w_in, conv_dim, conv_dim, "glu", F32,
                        biases=conv_b_in[o].reshape(1, 2 * conv_dim))
            zc = _conv_mid(glu, conv_dw[o], conv_dw_b[o], conv_ln_g[o], conv_ln_b[o], tp, seq, dec_seq)
            z = _mm([zc], _Weight(conv_w_out, (o,)), 0, d, None, F32, residual=residual(conv_b_out[o]))
        x, h = _resid(None, z, None, mod, rows, l, 1, 1.0, alpha, ln_g[l, 1], ln_b[l, 1], (l, 2))

        a = _dual(h, _Weight(ffn_w_gate, (l, 1)), _Weight(ffn_w_up, (l, 1)), 0, d_ff, "swiglu", BF16,
                  pad_cols_to=_kacc_pad(d_ff))
        z = _mm_kacc(a, _Weight(ffn_w_down, (l, 1)), residual=_Residual(x, None, mod, rows, l, 2, 0.5, alpha))
        if l + 1 < depth:
            x, h = _resid(None, z, None, mod, rows, l, 2, 0.5, alpha, ln_g[l, 2], ln_b[l, 2], (l + 1, 0))
        else:
            args = (None, z, None, mod, rows, l, 2, 0.5, alpha, ln_g[l, 2], ln_b[l, 2], None)
            outs = (_resid(*args, row_range=(0, tp)), _resid(*args, row_range=(tp, n_dec * dec_seq)))

    y_prompt = outs[0].reshape(batch, seq, d)
    y_sample = outs[1].reshape(n_dec, dec_seq, d)
    new_cache_k = jnp.stack(new_k, axis=1)
    new_cache_v = jnp.stack(new_v, axis=1)
    return (y_prompt, y_sample, new_cache_k, new_cache_v)
```
